```python
import math
import jax, jax.numpy as jnp
from jax import lax
import numpy as np

D_MODEL = 1024
BATCH = 16
SEQ = 4096
DEPTH = 4

N_MIXERS = 2
N_ATTN_LAYERS = (DEPTH + 1) // 2
N_HGRN_LAYERS = DEPTH // 2

ATTN_HEADS = 8
ATTN_HEAD_DIM = 64
ATTN_V_DIM = 2 * ATTN_HEAD_DIM
ROPE_THETA = 10000.0
Q_BLOCK = 128
SUBLN_EPS = 1e-5
LAMBDA_STD = 0.1

HG_HEADS = 8
HG_KEY_DIM = D_MODEL // HG_HEADS
HG_VAL_DIM = D_MODEL // HG_HEADS
CHUNK = 16
GNORM_EPS = 1e-6

FFN_HIDDEN = ((8 * D_MODEL // 3 + 255) // 256) * 256
NORM_EPS = 1e-6
MAX_POS_OFFSET = 1024

kernel_name = "interleaved_diffattn_hgrn2_swiglu"

F32 = jnp.float32


def rms_norm(x, w, eps=NORM_EPS):
    xf = x.astype(F32)
    y = xf * lax.rsqrt(jnp.mean(xf * xf, axis=-1, keepdims=True) + eps)
    return (y * w.astype(F32)).astype(x.dtype)


def rope_tables(positions):
    inv_freq = 1.0 / (ROPE_THETA ** (jnp.arange(0, ATTN_HEAD_DIM, 2, dtype=F32) / ATTN_HEAD_DIM))
    ang = positions.astype(F32)[..., None] * inv_freq
    return jnp.cos(ang), jnp.sin(ang)


def apply_rope(t, cos, sin):
    tf = t.astype(F32)
    t1, t2 = jnp.split(tf, 2, axis=-1)
    c = cos[:, :, None, None, :]
    s = sin[:, :, None, None, :]
    return jnp.concatenate([t1 * c - t2 * s, t2 * c + t1 * s], axis=-1).astype(t.dtype)


def lambda_init_fn(layer_idx):
    return 0.8 - 0.6 * math.exp(-0.3 * layer_idx)


def diff_attention(h, cos, sin, w_in, w_out, lq1, lk1, lq2, lk2, subln_w, lambda_init):
    B, S, _ = h.shape
    qkv = h @ w_in
    q, k, v = jnp.split(qkv, [D_MODEL, 2 * D_MODEL], axis=-1)
    q = apply_rope(q.reshape(B, S, ATTN_HEADS, 2, ATTN_HEAD_DIM), cos, sin)
    k = apply_rope(k.reshape(B, S, ATTN_HEADS, 2, ATTN_HEAD_DIM), cos, sin)
    v = v.reshape(B, S, ATTN_HEADS, ATTN_V_DIM)
    lam = (jnp.exp(jnp.sum(lq1.astype(F32) * lk1.astype(F32)))
           - jnp.exp(jnp.sum(lq2.astype(F32) * lk2.astype(F32))) + lambda_init)
    scale = ATTN_HEAD_DIM ** -0.5
    n_blk = S // Q_BLOCK
    q_blocks = q.reshape(B, n_blk, Q_BLOCK, ATTN_HEADS, 2, ATTN_HEAD_DIM).transpose(1, 0, 2, 3, 4, 5)
    k_idx = jnp.arange(S)

    def block(args):
        qb, blk = args
        s = jnp.einsum('bqhcd,bkhcd->bhcqk', qb, k).astype(F32) * scale
        q_idx = blk * Q_BLOCK + jnp.arange(Q_BLOCK)
        causal = k_idx[None, :] <= q_idx[:, None]
        s = jnp.where(causal, s, jnp.finfo(F32).min)
        p = jax.nn.softmax(s, axis=-1)
        a = p[:, :, 0] - lam * p[:, :, 1]
        return jnp.einsum('bhqk,bkhe->bqhe', a.astype(v.dtype), v)

    o = lax.map(block, (q_blocks, jnp.arange(n_blk)))
    o = o.transpose(1, 0, 2, 3, 4).reshape(B, S, ATTN_HEADS, ATTN_V_DIM)
    o = rms_norm(o, subln_w, SUBLN_EPS).astype(F32) * (1.0 - lambda_init)
    return o.reshape(B, S, D_MODEL).astype(h.dtype) @ w_out


def hgrn2_mixer(h, w_in, w_out, gnorm_w, lb):
    B, S, _ = h.shape
    n_c = S // CHUNK
    q, fz, i, g = jnp.split(h @ w_in, 4, axis=-1)
    f = lb.astype(F32) + (1.0 - lb.astype(F32)) * jax.nn.sigmoid(fz.astype(F32))
    k = 1.0 - f
    logf = jnp.log(f)

    def to_chunks(t):
        return t.reshape(B, n_c, CHUNK, HG_HEADS, -1).transpose(0, 3, 1, 2, 4)

    qc = to_chunks(jax.nn.silu(q.astype(F32)))
    kc = to_chunks(k)
    vc = to_chunks(i.astype(F32))
    bc = jnp.cumsum(to_chunks(logf), axis=3)

    qd = qc * jnp.exp(bc)
    kd = kc * jnp.exp(-bc)
    causal = jnp.tril(jnp.ones((CHUNK, CHUNK), dtype=bool))
    A = jnp.where(causal, jnp.einsum('bhncK,bhnjK->bhncj', qd, kd), 0.0)
    o_intra = jnp.einsum('bhncj,bhnjv->bhncv', A, vc)

    b_last = bc[:, :, :, -1:, :]
    k_to_end = kc * jnp.exp(b_last - bc)
    chunk_decay = jnp.exp(b_last[:, :, :, 0, :])

    def step(state, xs):
        qd_n, kend_n, v_n, dec_n = xs
        o_n = jnp.einsum('bhcK,bhKv->bhcv', qd_n, state)
        state = dec_n[..., None] * state + jnp.einsum('bhcK,bhcv->bhKv', kend_n, v_n)
        return state, o_n

    xs = (jnp.moveaxis(qd, 2, 0), jnp.moveaxis(k_to_end, 2, 0),
          jnp.moveaxis(vc, 2, 0), jnp.moveaxis(chunk_decay, 2, 0))
    state0 = jnp.zeros((B, HG_HEADS, HG_KEY_DIM, HG_VAL_DIM), F32)
    _, o_inter = lax.scan(step, state0, xs)
    o = o_intra + jnp.moveaxis(o_inter, 0, 2)
    o = o.transpose(0, 2, 3, 1, 4).reshape(B, S, HG_HEADS, HG_VAL_DIM)
    o = rms_norm(o, gnorm_w, GNORM_EPS) * jax.nn.silu(g.astype(F32)).reshape(B, S, HG_HEADS, HG_VAL_DIM)
    return o.reshape(B, S, D_MODEL).astype(h.dtype) @ w_out


def swiglu(h, w_in, w_out):
    gate, up = jnp.split(h @ w_in, 2, axis=-1)
    return (jax.nn.silu(gate) * up) @ w_out


def setup_inputs(seed: int = 0) -> dict:
    key = jax.random.key(seed)
    ks = jax.random.split(key, 20)
    D = D_MODEL
    nrm = lambda k, shape, s: jax.random.normal(k, shape, F32) * s
    x = jax.random.normal(ks[0], (BATCH, SEQ, D), F32)
    start = jax.random.randint(ks[1], (BATCH, 1), 0, MAX_POS_OFFSET, dtype=jnp.int32)
    positions = (start + jnp.arange(SEQ, dtype=jnp.int32)[None, :]).astype(jnp.int32)
    return {
        "x": x,
        "positions": positions,
        "norm_mix_w": 1.0 + nrm(ks[2], (DEPTH, D), 0.02),
        "norm_ffn_w": 1.0 + nrm(ks[3], (DEPTH, D), 0.02),
        "final_norm_w": 1.0 + nrm(ks[4], (D,), 0.02),
        "attn_w_in": nrm(ks[5], (N_ATTN_LAYERS, D, 3 * D), D ** -0.5),
        "attn_w_out": nrm(ks[6], (N_ATTN_LAYERS, D, D), D ** -0.5),
        "attn_lambda_q1": nrm(ks[7], (N_ATTN_LAYERS, ATTN_HEAD_DIM), LAMBDA_STD),
        "attn_lambda_k1": nrm(ks[8], (N_ATTN_LAYERS, ATTN_HEAD_DIM), LAMBDA_STD),
        "attn_lambda_q2": nrm(ks[9], (N_ATTN_LAYERS, ATTN_HEAD_DIM), LAMBDA_STD),
        "attn_lambda_k2": nrm(ks[10], (N_ATTN_LAYERS, ATTN_HEAD_DIM), LAMBDA_STD),
        "attn_subln_w": 1.0 + nrm(ks[11], (N_ATTN_LAYERS, ATTN_V_DIM), 0.02),
        "hgrn_w_in": nrm(ks[12], (N_HGRN_LAYERS, D, 4 * D), D ** -0.5),
        "hgrn_w_out": nrm(ks[13], (N_HGRN_LAYERS, D, D), D ** -0.5),
        "hgrn_gnorm_w": 1.0 + nrm(ks[14], (N_HGRN_LAYERS, HG_VAL_DIM), 0.02),
        "hgrn_lb_param": nrm(ks[15], (DEPTH, D), 0.1),
        "ffn_w_in": nrm(ks[16], (DEPTH, D, 2 * FFN_HIDDEN), D ** -0.5),
        "ffn_w_out": nrm(ks[17], (DEPTH, FFN_HIDDEN, D), FFN_HIDDEN ** -0.5),
    }


def reference(x, positions, norm_mix_w, norm_ffn_w, final_norm_w,
              attn_w_in, attn_w_out, attn_lambda_q1, attn_lambda_k1, attn_lambda_q2, attn_lambda_k2,
              attn_subln_w, hgrn_w_in, hgrn_w_out, hgrn_gnorm_w, hgrn_lb_param,
              ffn_w_in, ffn_w_out):
    cos, sin = rope_tables(positions)
    lbs = jnp.cumsum(jax.nn.softmax(hgrn_lb_param.astype(F32), axis=0), axis=0)
    lbs = lbs - lbs[0:1]
    for i in range(DEPTH):
        h = rms_norm(x, norm_mix_w[i])
        j = i // N_MIXERS
        if i % N_MIXERS == 0:
            y = diff_attention(h, cos, sin, attn_w_in[j], attn_w_out[j],
                               attn_lambda_q1[j], attn_lambda_k1[j], attn_lambda_q2[j], attn_lambda_k2[j],
                               attn_subln_w[j], lambda_init_fn(i))
        else:
            y = hgrn2_mixer(h, hgrn_w_in[j], hgrn_w_out[j], hgrn_gnorm_w[j], lbs[i])
        x = x + y.astype(x.dtype)
        x = x + swiglu(rms_norm(x, norm_ffn_w[i]), ffn_w_in[i], ffn_w_out[i]).astype(x.dtype)
    return rms_norm(x, final_norm_w)
```

```python
import functools
import math

import jax
import jax.numpy as jnp
from jax import lax
from jax.experimental import pallas as pl
from jax.experimental.pallas import tpu as pltpu

F32 = jnp.float32
BF16 = jnp.bfloat16

D_MODEL = 1024
HEADS = 8
HEAD_W = 128
MAP_DIM = 64
ROPE_HALF = MAP_DIM // 2
ROPE_THETA = 10000.0
FFN_HIDDEN = 2816
NORM_EPS = 1e-6
SUBLN_EPS = 1e-5
GNORM_EPS = 1e-6

ROW_TILE = 512
COL_CHUNK = 512
FFN_CHUNK = 256
ATTN_TQ = 256
ATTN_TK = 256
HGRN_CHUNK = 128
VMEM_LIMIT = 56 * 1024 * 1024
MASK_VALUE = float(jnp.finfo(jnp.float32).min)


def _rms_scale(x, eps):
    return lax.rsqrt(jnp.mean(x * x, axis=-1, keepdims=True) + eps)


def _const_spec(shape):
    return pl.BlockSpec(shape, lambda *_: (0,) * len(shape), pipeline_mode=pl.Buffered(1))


def _rope_table_kernel(pos_ref, invf_ref, cos_ref, sin_ref):
    ang = pos_ref[...].astype(F32) * invf_ref[...]
    lane = lax.broadcasted_iota(jnp.int32, ang.shape, 1)
    first_half = (lane & ROPE_HALF) == 0
    s = jnp.sin(ang)
    cos_ref[...] = jnp.cos(ang)
    sin_ref[...] = jnp.where(first_half, -s, s)


def _rope_tables(positions):
    t = positions.size
    inv_freq = 1.0 / (ROPE_THETA ** (jnp.arange(0, MAP_DIM, 2, dtype=F32) / MAP_DIM))
    invf = jnp.tile(inv_freq, HEAD_W // ROPE_HALF).reshape(1, HEAD_W)
    tm = 1024
    return pl.pallas_call(
        _rope_table_kernel,
        out_shape=(jax.ShapeDtypeStruct((t, HEAD_W), F32),) * 2,
        grid=(t // tm,),
        in_specs=[pl.BlockSpec((tm, 1), lambda i: (i, 0)), _const_spec((1, HEAD_W))],
        out_specs=(pl.BlockSpec((tm, HEAD_W), lambda i: (i, 0)),) * 2,
        name="rope_tables",
    )(positions.reshape(t, 1), invf)


def _proj_kernel(x_ref, nw_ref, w_ref, *rest, rope_cols, q_cols, q_scale):
    if rope_cols:
        cos_ref, sin_ref, o_ref = rest
    else:
        (o_ref,) = rest
    x = x_ref[...]
    xn = (x * _rms_scale(x, NORM_EPS) * nw_ref[...]).astype(BF16)
    n_out = o_ref.shape[1]
    if rope_cols:
        cos = cos_ref[...]
        sin = sin_ref[...]
        lane = lax.broadcasted_iota(jnp.int32, cos.shape, 1)
        first_half = (lane & ROPE_HALF) == 0
    for c0 in range(0, n_out, COL_CHUNK):
        y = jnp.dot(xn, w_ref[:, c0:c0 + COL_CHUNK], preferred_element_type=F32)
        if c0 < rope_cols:
            for g0 in range(0, COL_CHUNK, HEAD_W):
                yg = y[:, g0:g0 + HEAD_W]
                rot = jnp.where(first_half, pltpu.roll(yg, HEAD_W - ROPE_HALF, 1),
                                pltpu.roll(yg, ROPE_HALF, 1))
                r = yg * cos + rot * sin
                if c0 < q_cols:
                    r = r * q_scale
                o_ref[:, c0 + g0:c0 + g0 + HEAD_W] = r.astype(o_ref.dtype)
        else:
            o_ref[:, c0:c0 + COL_CHUNK] = y.astype(o_ref.dtype)


def _project(x2, norm_w, w, rope=None):
    t, d = x2.shape
    n_out = w.shape[1]
    in_specs = [pl.BlockSpec((ROW_TILE, d), lambda i: (i, 0)),
                _const_spec((1, d)), _const_spec((d, n_out))]
    args = [x2, norm_w.reshape(1, d), w]
    if rope is not None:
        in_specs += [pl.BlockSpec((ROW_TILE, HEAD_W), lambda i: (i, 0))] * 2
        args += list(rope)
    kern = functools.partial(
        _proj_kernel, rope_cols=2 * d if rope is not None else 0,
        q_cols=d, q_scale=MAP_DIM ** -0.5)
    return pl.pallas_call(
        kern,
        out_shape=jax.ShapeDtypeStruct((t, n_out), BF16),
        grid=(t // ROW_TILE,),
        in_specs=in_specs,
        out_specs=pl.BlockSpec((ROW_TILE, n_out), lambda i: (i, 0)),
        compiler_params=pltpu.CompilerParams(
            dimension_semantics=("parallel",), vmem_limit_bytes=VMEM_LIMIT),
        name="norm_proj_rope" if rope is not None else "norm_proj",
    )(*args)


def _attn_kernel(q_ref, k_ref, v_ref, lam_ref, sw_ref, o_ref, *, lambda_init):
    tq, tk = ATTN_TQ, ATTN_TK
    qi = pl.program_id(2)
    q = q_ref[...]
    lane = lax.broadcasted_iota(jnp.int32, q.shape, 1)
    zero = jnp.zeros_like(q)
    qq = jnp.concatenate([jnp.where(lane < MAP_DIM, q, zero),
                          jnp.where(lane >= MAP_DIM, q, zero)], axis=0)

    def step(j, carry, masked):
        m, l, acc = carry
        start = pl.multiple_of(j * tk, tk)
        k = k_ref[pl.ds(start, tk), :]
        v = v_ref[pl.ds(start, tk), :]
        s = lax.dot_general(qq, k, (((1,), (1,)), ((), ())), preferred_element_type=F32)
        if masked:
            row = (lax.broadcasted_iota(jnp.int32, (2 * tq, tk), 0) & (tq - 1)) + qi * tq
            col = lax.broadcasted_iota(jnp.int32, (2 * tq, tk), 1) + j * tk
            s = jnp.where(col <= row, s, MASK_VALUE)
        m_new = jnp.maximum(m, jnp.max(s, axis=-1, keepdims=True))
        alpha = jnp.exp(m - m_new)
        p = jnp.exp(s - m_new)
        l = alpha * l + jnp.sum(p, axis=-1, keepdims=True)
        acc = alpha * acc + jnp.dot(p.astype(BF16), v, preferred_element_type=F32)
        return m_new, l, acc

    init = (jnp.full((2 * tq, 1), -1e30, F32), jnp.zeros((2 * tq, 1), F32),
            jnp.zeros((2 * tq, HEAD_W), F32))
    n_full = qi * (tq // tk)
    carry = lax.fori_loop(0, n_full, functools.partial(step, masked=False), init)
    for d in range(tq // tk):
        carry = step(n_full + d, carry, True)
    _, l, acc = carry

    lam_p = lam_ref[...]
    lam = (jnp.exp(jnp.sum(lam_p[0:1] * lam_p[1:2], axis=-1, keepdims=True))
           - jnp.exp(jnp.sum(lam_p[2:3] * lam_p[3:4], axis=-1, keepdims=True)) + lambda_init)
    o_maps = acc / l
    o = o_maps[:tq] - lam * o_maps[tq:]
    o = o * _rms_scale(o, SUBLN_EPS) * sw_ref[...] * (1.0 - lambda_init)
    o_ref[...] = o.astype(o_ref.dtype)


def _diff_attention(qkv3, lam_params, subln_w, lambda_init):
    b, s, _ = qkv3.shape
    kern = functools.partial(_attn_kernel, lambda_init=lambda_init)
    return pl.pallas_call(
        kern,
        out_shape=jax.ShapeDtypeStruct((b, s, D_MODEL), BF16),
        grid=(b, HEADS, s // ATTN_TQ),
        in_specs=[
            pl.BlockSpec((None, ATTN_TQ, HEAD_W), lambda bi, h, i: (bi, i, h)),
            pl.BlockSpec((None, s, HEAD_W), lambda bi, h, i: (bi, 0, HEADS + h)),
            pl.BlockSpec((None, s, HEAD_W), lambda bi, h, i: (bi, 0, 2 * HEADS + h)),
            _const_spec((4, MAP_DIM)),
            _const_spec((1, HEAD_W)),
        ],
        out_specs=pl.BlockSpec((None, ATTN_TQ, HEAD_W), lambda bi, h, i: (bi, i, h)),
        compiler_params=pltpu.CompilerParams(
            dimension_semantics=("parallel", "parallel", "arbitrary"),
            vmem_limit_bytes=VMEM_LIMIT),
        name="diff_flash_attention",
    )(qkv3, qkv3, qkv3, lam_params, subln_w.reshape(1, HEAD_W))


def _split3(x):
    hi = x.astype(BF16)
    r = x - hi.astype(F32)
    mid = r.astype(BF16)
    lo = (r - mid.astype(F32)).astype(BF16)
    return hi, mid, lo


def _hgrn_kernel(q_ref, f_ref, i_ref, g_ref, lbp_ref, gw_ref, o_ref, state_ref, lb_ref, *, layer):
    c = HGRN_CHUNK

    @pl.when(pl.program_id(1) == 0)
    def _():
        state_ref[...] = jnp.zeros_like(state_ref)
        p = lbp_ref[...]
        e = jnp.exp(p - jnp.max(p, axis=0, keepdims=True))
        lb_ref[...] = (jnp.sum(e[1:layer + 1], axis=0, keepdims=True)
                       / jnp.sum(e, axis=0, keepdims=True))

    lb = lb_ref[...]
    f = lb + (1.0 - lb) * jax.nn.sigmoid(f_ref[...].astype(F32))
    kk = 1.0 - f
    logf = jnp.log(f)
    r_idx = lax.broadcasted_iota(jnp.int32, (c, c), 0)
    c_idx = lax.broadcasted_iota(jnp.int32, (c, c), 1)
    causal = c_idx <= r_idx
    tri = causal.astype(BF16)
    cum = sum(jnp.dot(tri, part, preferred_element_type=F32) for part in _split3(logf))
    cum_mid = cum[c // 2 - 1:c // 2]
    cum_last = cum[c - 1:c]
    qs = jax.nn.silu(q_ref[...].astype(F32))
    q_state = (qs * jnp.exp(cum)).astype(BF16)
    q_intra = (qs * jnp.exp(cum - cum_mid)).astype(BF16)
    k_intra = (kk * jnp.exp(cum_mid - cum)).astype(BF16)
    k_end = (kk * jnp.exp(cum_last - cum)).astype(BF16)
    chunk_decay = jnp.exp(cum_last)
    v = i_ref[...]
    g = jax.nn.silu(g_ref[...].astype(F32))
    gw = gw_ref[...]
    nt = (((1,), (1,)), ((), ()))
    for h in range(HEADS):
        sl = slice(h * HEAD_W, (h + 1) * HEAD_W)
        st = state_ref[h]
        a = lax.dot_general(q_intra[:, sl], k_intra[:, sl], nt, preferred_element_type=F32)
        a = jnp.where(causal, a, 0.0).astype(BF16)
        o = (jnp.dot(a, v[:, sl], preferred_element_type=F32)
             + lax.dot_general(q_state[:, sl], st.astype(BF16), nt, preferred_element_type=F32))
        vt = v[:, sl].astype(F32).T.astype(BF16)
        state_ref[h] = chunk_decay[:, sl] * st + jnp.dot(vt, k_end[:, sl], preferred_element_type=F32)
        o = o * _rms_scale(o, GNORM_EPS) * gw * g[:, sl]
        o_ref[:, sl] = o.astype(o_ref.dtype)


def _hgrn_recurrence(u3, lb_param, gnorm_w, layer):
    b, s, _ = u3.shape
    c = HGRN_CHUNK
    sec = lambda k: pl.BlockSpec((None, c, D_MODEL), lambda bi, n: (bi, n, k))
    return pl.pallas_call(
        functools.partial(_hgrn_kernel, layer=layer),
        out_shape=jax.ShapeDtypeStruct((b, s, D_MODEL), BF16),
        grid=(b, s // c),
        in_specs=[sec(0), sec(1), sec(2), sec(3), _const_spec(lb_param.shape),
                  _const_spec((1, HEAD_W))],
        out_specs=pl.BlockSpec((None, c, D_MODEL), lambda bi, n: (bi, n, 0)),
        scratch_shapes=[pltpu.VMEM((HEADS, HEAD_W, HEAD_W), F32), pltpu.VMEM((1, D_MODEL), F32)],
        compiler_params=pltpu.CompilerParams(
            dimension_semantics=("parallel", "arbitrary"), vmem_limit_bytes=VMEM_LIMIT),
        name="hgrn2_recurrence",
    )(u3, u3, u3, u3, lb_param.astype(F32), gnorm_w.reshape(1, HEAD_W))


def _post_kernel(y_ref, x_ref, wo_ref, nw_ref, win_ref, wout_ref, *rest, final_norm):
    if final_norm:
        fw_ref, o_ref, h_ref = rest
    else:
        o_ref, h_ref = rest
    x = x_ref[...] + jnp.dot(y_ref[...], wo_ref[...], preferred_element_type=F32)
    xn = (x * _rms_scale(x, NORM_EPS) * nw_ref[...]).astype(BF16)
    for c0 in range(0, FFN_HIDDEN, FFN_CHUNK):
        gate = jnp.dot(xn, win_ref[:, c0:c0 + FFN_CHUNK], preferred_element_type=F32)
        up = jnp.dot(xn, win_ref[:, FFN_HIDDEN + c0:FFN_HIDDEN + c0 + FFN_CHUNK],
                     preferred_element_type=F32)
        h_ref[:, c0:c0 + FFN_CHUNK] = (jax.nn.silu(gate) * up).astype(BF16)
    x = x + jnp.dot(h_ref[...], wout_ref[...], preferred_element_type=F32)
    if final_norm:
        x = x * _rms_scale(x, NORM_EPS) * fw_ref[...]
    o_ref[...] = x


def _post_mixer(y2, x2, w_o, norm_w, w_in, w_out, final_w=None):
    t, d = x2.shape
    row = lambda n: pl.BlockSpec((ROW_TILE, n), lambda i: (i, 0))
    in_specs = [row(d), row(d), _const_spec((d, d)), _const_spec((1, d)),
                _const_spec((d, 2 * FFN_HIDDEN)), _const_spec((FFN_HIDDEN, d))]
    args = [y2, x2, w_o, norm_w.reshape(1, d), w_in, w_out]
    if final_w is not None:
        in_specs.append(_const_spec((1, d)))
        args.append(final_w.reshape(1, d))
    return pl.pallas_call(
        functools.partial(_post_kernel, final_norm=final_w is not None),
        out_shape=jax.ShapeDtypeStruct((t, d), F32),
        grid=(t // ROW_TILE,),
        in_specs=in_specs,
        out_specs=row(d),
        scratch_shapes=[pltpu.VMEM((ROW_TILE, FFN_HIDDEN), BF16)],
        compiler_params=pltpu.CompilerParams(
            dimension_semantics=("parallel",), vmem_limit_bytes=VMEM_LIMIT),
        name="outproj_ffn",
    )(*args)


def _lambda_init(layer_idx):
    return 0.8 - 0.6 * math.exp(-0.3 * layer_idx)


def kernel(x, positions, norm_mix_w, norm_ffn_w, final_norm_w, attn_w_in, attn_w_out, attn_lambda_q1, attn_lambda_k1, attn_lambda_q2, attn_lambda_k2, attn_subln_w, hgrn_w_in, hgrn_w_out, hgrn_gnorm_w, hgrn_lb_param, ffn_w_in, ffn_w_out):
    b, s, d = x.shape
    depth = norm_mix_w.shape[0]
    x2 = x.reshape(b * s, d)
    rope = _rope_tables(positions)
    for i in range(depth):
        j = i // 2
        last = final_norm_w if i == depth - 1 else None
        if i % 2 == 0:
            qkv = _project(x2, norm_mix_w[i], attn_w_in[j].astype(BF16), rope=rope)
            lam_params = jnp.stack([attn_lambda_q1[j], attn_lambda_k1[j],
                                    attn_lambda_q2[j], attn_lambda_k2[j]]).astype(F32)
            y = _diff_attention(qkv.reshape(b, s, 3 * d), lam_params, attn_subln_w[j], _lambda_init(i))
            w_o = attn_w_out[j]
        else:
            u = _project(x2, norm_mix_w[i], hgrn_w_in[j].astype(BF16))
            y = _hgrn_recurrence(u.reshape(b, s, 4 * d), hgrn_lb_param, hgrn_gnorm_w[j], i)
            w_o = hgrn_w_out[j]
        x2 = _post_mixer(y.reshape(b * s, d), x2, w_o.astype(BF16), norm_ffn_w[i],
                         ffn_w_in[i].astype(BF16), ffn_w_out[i].astype(BF16), last)
    return x2.reshape(b, s, d)
```

```python
import functools
import math

import jax
import jax.numpy as jnp
from jax import lax
from jax.experimental import pallas as pl
from jax.experimental.pallas import tpu as pltpu

F32 = jnp.float32
BF16 = jnp.bfloat16

D_MODEL = 1024
HEADS = 8
HEAD_W = 128
MAP_DIM = 64
ROPE_HALF = MAP_DIM // 2
ROPE_THETA = 10000.0
FFN_HIDDEN = 2816
NORM_EPS = 1e-6
SUBLN_EPS = 1e-5
GNORM_EPS = 1e-6

ROW_TILE = 512
COL_CHUNK = 512
FFN_CHUNK = 256
ATTN_BLOCK = 256
ATTN_UNROLL = 4
HGRN_CHUNK = 128
VMEM_LIMIT = 56 * 1024 * 1024
MASK_VALUE = float(jnp.finfo(jnp.float32).min)


def _rms_scale(x, eps):
    return lax.rsqrt(jnp.mean(x * x, axis=-1, keepdims=True) + eps)


def _const_spec(shape):
    return pl.BlockSpec(shape, lambda *_: (0,) * len(shape), pipeline_mode=pl.Buffered(1))


def _rope_table_kernel(pos_ref, invf_ref, cos_ref, sin_ref):
    ang = pos_ref[...].astype(F32) * invf_ref[...]
    lane = lax.broadcasted_iota(jnp.int32, ang.shape, 1)
    first_half = (lane & ROPE_HALF) == 0
    s = jnp.sin(ang)
    cos_ref[...] = jnp.cos(ang)
    sin_ref[...] = jnp.where(first_half, -s, s)


def _rope_tables(positions):
    t = positions.size
    inv_freq = 1.0 / (ROPE_THETA ** (jnp.arange(0, MAP_DIM, 2, dtype=F32) / MAP_DIM))
    invf = jnp.tile(inv_freq, HEAD_W // ROPE_HALF).reshape(1, HEAD_W)
    tm = ROW_TILE
    return pl.pallas_call(
        _rope_table_kernel,
        out_shape=(jax.ShapeDtypeStruct((t, HEAD_W), F32),) * 2,
        grid=(t // tm,),
        in_specs=[pl.BlockSpec((tm, 1), lambda i: (i, 0)), _const_spec((1, HEAD_W))],
        out_specs=(pl.BlockSpec((tm, HEAD_W), lambda i: (i, 0)),) * 2,
        name="rope_tables",
    )(positions.reshape(t, 1), invf)


def _proj_kernel(x_ref, nw_ref, w_ref, *rest, rope_cols, q_cols, q_scale):
    if rope_cols:
        cos_ref, sin_ref, o_ref = rest
    else:
        (o_ref,) = rest
    x = x_ref[...]
    xn = (x * _rms_scale(x, NORM_EPS) * nw_ref[...]).astype(BF16)
    n_out = o_ref.shape[1]
    if rope_cols:
        cos = cos_ref[...]
        sin = sin_ref[...]
        lane = lax.broadcasted_iota(jnp.int32, cos.shape, 1)
        first_half = (lane & ROPE_HALF) == 0
    for c0 in range(0, n_out, COL_CHUNK):
        y = jnp.dot(xn, w_ref[:, c0:c0 + COL_CHUNK], preferred_element_type=F32)
        if c0 < rope_cols:
            for g0 in range(0, COL_CHUNK, HEAD_W):
                yg = y[:, g0:g0 + HEAD_W]
                rot = jnp.where(first_half, pltpu.roll(yg, HEAD_W - ROPE_HALF, 1),
                                pltpu.roll(yg, ROPE_HALF, 1))
                r = yg * cos + rot * sin
                if c0 < q_cols:
                    r = r * q_scale
                o_ref[:, c0 + g0:c0 + g0 + HEAD_W] = r.astype(o_ref.dtype)
        else:
            o_ref[:, c0:c0 + COL_CHUNK] = y.astype(o_ref.dtype)


def _project(x2, norm_w, w, rope=None):
    t, d = x2.shape
    n_out = w.shape[1]
    in_specs = [pl.BlockSpec((ROW_TILE, d), lambda i: (i, 0)),
                _const_spec((1, d)), _const_spec((d, n_out))]
    args = [x2, norm_w.reshape(1, d), w]
    if rope is not None:
        in_specs += [pl.BlockSpec((ROW_TILE, HEAD_W), lambda i: (i, 0))] * 2
        args += list(rope)
    kern = functools.partial(
        _proj_kernel, rope_cols=2 * d if rope is not None else 0,
        q_cols=d, q_scale=MAP_DIM ** -0.5 * math.log2(math.e))
    return pl.pallas_call(
        kern,
        out_shape=jax.ShapeDtypeStruct((t, n_out), BF16),
        grid=(t // ROW_TILE,),
        in_specs=in_specs,
        out_specs=pl.BlockSpec((ROW_TILE, n_out), lambda i: (i, 0)),
        compiler_params=pltpu.CompilerParams(
            dimension_semantics=("parallel",), vmem_limit_bytes=VMEM_LIMIT),
        name="norm_proj_rope" if rope is not None else "norm_proj",
    )(*args)


def _attn_kernel(qi_tab, kj_tab, q_ref, k_ref, v_ref, lam_ref, sw_ref, o_ref,
                 qm_ref, m_ref, acc_ref, s0_ref, s1_ref, p0_ref, p1_ref, a0_ref, a1_ref,
                 *, lambda_init, n_full, n_diag):
    t = ATTN_BLOCK
    n_q = q_ref.shape[0] // t
    nt = (((1,), (1,)), ((), ()))
    s_refs, p_refs, a_refs = (s0_ref, s1_ref), (p0_ref, p1_ref), (a0_ref, a1_ref)

    def split_maps(i, _):
        rows = pl.ds(pl.multiple_of(i * t, t), t)
        q = q_ref[rows, :]
        lane = lax.broadcasted_iota(jnp.int32, q.shape, 1)
        zero = jnp.zeros_like(q)
        qm_ref[0, rows, :] = jnp.where(lane < MAP_DIM, q, zero)
        qm_ref[1, rows, :] = jnp.where(lane >= MAP_DIM, q, zero)
        m_ref[i] = jnp.full(m_ref.shape[1:], -1e30, F32)
        acc_ref[i] = jnp.zeros(acc_ref.shape[1:], F32)
        return 0

    lax.fori_loop(0, n_q, split_maps, 0)
    ones = jnp.ones((t, HEAD_W), BF16)
    tril = (lax.broadcasted_iota(jnp.int32, (t, t), 1)
            <= lax.broadcasted_iota(jnp.int32, (t, t), 0))

    def scores(b, slot):
        rows = pl.ds(pl.multiple_of(qi_tab[b] * t, t), t)
        k = k_ref[pl.ds(pl.multiple_of(kj_tab[b] * t, t), t), :]
        for c in range(2):
            s_refs[slot][c] = lax.dot_general(qm_ref[c, rows, :], k, nt, preferred_element_type=F32)

    def softmax(b, slot, masked):
        qi = qi_tab[b]
        for c in range(2):
            sc = s_refs[slot][c]
            if masked:
                sc = jnp.where(tril, sc, MASK_VALUE)
            halves = [sc[:, g:g + HEAD_W] for g in range(0, t, HEAD_W)]
            m = m_ref[qi, c]
            m_new = jnp.maximum(m, jnp.max(functools.reduce(jnp.maximum, halves),
                                           axis=-1, keepdims=True))
            m_ref[qi, c] = m_new
            a_refs[slot][c] = jnp.exp2(m - m_new)
            p_refs[slot][c] = jnp.concatenate(
                [jnp.exp2(h - m_new).astype(BF16) for h in halves], axis=1)

    def values(b, slot):
        qi = qi_tab[b]
        v_aug = jnp.concatenate([v_ref[pl.ds(pl.multiple_of(kj_tab[b] * t, t), t), :], ones], axis=1)
        for c in range(2):
            pv = jnp.dot(p_refs[slot][c], v_aug, preferred_element_type=F32)
            alpha = a_refs[slot][c]
            for g in range(0, 2 * HEAD_W, HEAD_W):
                acc_ref[qi, c, :, g:g + HEAD_W] = (alpha * acc_ref[qi, c, :, g:g + HEAD_W]
                                                   + pv[:, g:g + HEAD_W])

    def run(first, count, masked):
        if count % 2:
            last = first + count - 1
            scores(last, 0)
            softmax(last, 0, masked)
            values(last, 0)
            count -= 1
        if count == 0:
            return

        def tick(b, parity):
            scores(b, parity)
            softmax(b - 1, 1 - parity, masked)
            values(b - 2, parity)

        scores(first, 0)
        scores(first + 1, 1)
        softmax(first, 0, masked)
        unroll = ATTN_UNROLL
        n_iter = (count - 2) // unroll

        def body(i, _):
            for u in range(unroll):
                tick(first + 2 + unroll * i + u, u % 2)
            return 0

        lax.fori_loop(0, n_iter, body, 0)
        for b in range(first + 2 + unroll * n_iter, first + count):
            tick(b, (b - first) % 2)
        last = first + count - 1
        softmax(last, 1, masked)
        values(last - 1, 0)
        values(last, 1)

    run(0, n_full, False)
    run(n_full, n_diag, True)

    lam_p = lam_ref[...]
    lam = (jnp.exp(jnp.sum(lam_p[0:1] * lam_p[1:2], axis=-1, keepdims=True))
           - jnp.exp(jnp.sum(lam_p[2:3] * lam_p[3:4], axis=-1, keepdims=True)) + lambda_init)
    sw = sw_ref[...] * (1.0 - lambda_init)

    def finish(i, _):
        acc0, acc1 = acc_ref[i, 0], acc_ref[i, 1]
        o = acc0[:, :HEAD_W] / acc0[:, HEAD_W:] - lam * (acc1[:, :HEAD_W] / acc1[:, HEAD_W:])
        o = o * _rms_scale(o, SUBLN_EPS) * sw
        o_ref[pl.ds(pl.multiple_of(i * t, t), t), :] = o.astype(o_ref.dtype)
        return 0

    lax.fori_loop(0, n_q, finish, 0)


def _diff_attention(qkv3, lam_params, subln_w, lambda_init):
    b, s, _ = qkv3.shape
    t = ATTN_BLOCK
    n_q = s // t
    pairs = [(i, j) for i in range(n_q) for j in range(i)] + [(i, i) for i in range(n_q)]
    n_diag, n_full = n_q, len(pairs) - n_q
    qi_tab = jnp.asarray([p[0] for p in pairs], jnp.int32)
    kj_tab = jnp.asarray([p[1] for p in pairs], jnp.int32)
    kern = functools.partial(_attn_kernel, lambda_init=lambda_init, n_full=n_full, n_diag=n_diag)
    head = lambda col0: pl.BlockSpec((None, s, HEAD_W), lambda bi, h, *_: (bi, 0, col0 + h))
    pair_buf = lambda n, dt: [pltpu.VMEM((2, t, n), dt)] * 2
    return pl.pallas_call(
        kern,
        out_shape=jax.ShapeDtypeStruct((b, s, D_MODEL), BF16),
        grid_spec=pltpu.PrefetchScalarGridSpec(
            num_scalar_prefetch=2,
            grid=(b, HEADS),
            in_specs=[head(0), head(HEADS), head(2 * HEADS),
                      _const_spec((4, MAP_DIM)), _const_spec((1, HEAD_W))],
            out_specs=head(0),
            scratch_shapes=[
                pltpu.VMEM((2, s, HEAD_W), BF16),
                pltpu.VMEM((n_q, 2, t, HEAD_W), F32),
                pltpu.VMEM((n_q, 2, t, 2 * HEAD_W), F32),
                *pair_buf(t, F32), *pair_buf(t, BF16), *pair_buf(HEAD_W, F32)],
        ),
        compiler_params=pltpu.CompilerParams(
            dimension_semantics=("parallel", "parallel"), vmem_limit_bytes=VMEM_LIMIT),
        name="diff_flash_attention",
    )(qi_tab, kj_tab, qkv3, qkv3, qkv3, lam_params, subln_w.reshape(1, HEAD_W))


def _split3(x):
    hi = x.astype(BF16)
    r = x - hi.astype(F32)
    mid = r.astype(BF16)
    lo = (r - mid.astype(F32)).astype(BF16)
    return hi, mid, lo


def _hgrn_kernel(q_ref, f_ref, i_ref, g_ref, lbp_ref, gw_ref, o_ref, state_ref, lb_ref, *, layer):
    c = HGRN_CHUNK

    @pl.when(pl.program_id(1) == 0)
    def _():
        state_ref[...] = jnp.zeros_like(state_ref)
        p = lbp_ref[...]
        e = jnp.exp(p - jnp.max(p, axis=0, keepdims=True))
        lb_ref[...] = (jnp.sum(e[1:layer + 1], axis=0, keepdims=True)
                       / jnp.sum(e, axis=0, keepdims=True))

    lb = lb_ref[...]
    f = lb + (1.0 - lb) * jax.nn.sigmoid(f_ref[...].astype(F32))
    kk = 1.0 - f
    logf = jnp.log(f)
    r_idx = lax.broadcasted_iota(jnp.int32, (c, c), 0)
    c_idx = lax.broadcasted_iota(jnp.int32, (c, c), 1)
    causal = c_idx <= r_idx
    tri = causal.astype(BF16)
    cum = sum(jnp.dot(tri, part, preferred_element_type=F32) for part in _split3(logf))
    cum_mid = cum[c // 2 - 1:c // 2]
    cum_last = cum[c - 1:c]
    qs = jax.nn.silu(q_ref[...].astype(F32))
    q_state = (qs * jnp.exp(cum)).astype(BF16)
    q_intra = (qs * jnp.exp(cum - cum_mid)).astype(BF16)
    k_intra = (kk * jnp.exp(cum_mid - cum)).astype(BF16)
    k_end = (kk * jnp.exp(cum_last - cum)).astype(BF16)
    chunk_decay = jnp.exp(cum_last)
    v = i_ref[...]
    g = jax.nn.silu(g_ref[...].astype(F32))
    gw = gw_ref[...]
    nt = (((1,), (1,)), ((), ()))
    for h in range(HEADS):
        sl = slice(h * HEAD_W, (h + 1) * HEAD_W)
        st = state_ref[h]
        a = lax.dot_general(q_intra[:, sl], k_intra[:, sl], nt, preferred_element_type=F32)
        a = jnp.where(causal, a, 0.0).astype(BF16)
        o = (jnp.dot(a, v[:, sl], preferred_element_type=F32)
             + lax.dot_general(q_state[:, sl], st.astype(BF16), nt, preferred_element_type=F32))
        vt = v[:, sl].astype(F32).T.astype(BF16)
        state_ref[h] = chunk_decay[:, sl] * st + jnp.dot(vt, k_end[:, sl], preferred_element_type=F32)
        o = o * _rms_scale(o, GNORM_EPS) * gw * g[:, sl]
        o_ref[:, sl] = o.astype(o_ref.dtype)


def _hgrn_recurrence(u3, lb_param, gnorm_w, layer):
    b, s, _ = u3.shape
    c = HGRN_CHUNK
    sec = lambda k: pl.BlockSpec((None, c, D_MODEL), lambda bi, n: (bi, n, k))
    return pl.pallas_call(
        functools.partial(_hgrn_kernel, layer=layer),
        out_shape=jax.ShapeDtypeStruct((b, s, D_MODEL), BF16),
        grid=(b, s // c),
        in_specs=[sec(0), sec(1), sec(2), sec(3), _const_spec(lb_param.shape),
                  _const_spec((1, HEAD_W))],
        out_specs=pl.BlockSpec((None, c, D_MODEL), lambda bi, n: (bi, n, 0)),
        scratch_shapes=[pltpu.VMEM((HEADS, HEAD_W, HEAD_W), F32), pltpu.VMEM((1, D_MODEL), F32)],
        compiler_params=pltpu.CompilerParams(
            dimension_semantics=("parallel", "arbitrary"), vmem_limit_bytes=VMEM_LIMIT),
        name="hgrn2_recurrence",
    )(u3, u3, u3, u3, lb_param.astype(F32), gnorm_w.reshape(1, HEAD_W))


def _post_kernel(y_ref, x_ref, wo_ref, nw_ref, win_ref, wout_ref, *rest, final_norm):
    if final_norm:
        fw_ref, o_ref, h_ref = rest
    else:
        o_ref, h_ref = rest
    x = x_ref[...] + jnp.dot(y_ref[...], wo_ref[...], preferred_element_type=F32)
    xn = (x * _rms_scale(x, NORM_EPS) * nw_ref[...]).astype(BF16)
    for c0 in range(0, FFN_HIDDEN, FFN_CHUNK):
        gate = jnp.dot(xn, win_ref[:, c0:c0 + FFN_CHUNK], preferred_element_type=F32)
        up = jnp.dot(xn, win_ref[:, FFN_HIDDEN + c0:FFN_HIDDEN + c0 + FFN_CHUNK],
                     preferred_element_type=F32)
        h_ref[:, c0:c0 + FFN_CHUNK] = (jax.nn.silu(gate) * up).astype(BF16)
    x = x + jnp.dot(h_ref[...], wout_ref[...], preferred_element_type=F32)
    if final_norm:
        x = x * _rms_scale(x, NORM_EPS) * fw_ref[...]
    o_ref[...] = x


def _post_mixer(y2, x2, w_o, norm_w, w_in, w_out, final_w=None):
    t, d = x2.shape
    row = lambda n: pl.BlockSpec((ROW_TILE, n), lambda i: (i, 0))
    in_specs = [row(d), row(d), _const_spec((d, d)), _const_spec((1, d)),
                _const_spec((d, 2 * FFN_HIDDEN)), _const_spec((FFN_HIDDEN, d))]
    args = [y2, x2, w_o, norm_w.reshape(1, d), w_in, w_out]
    if final_w is not None:
        in_specs.append(_const_spec((1, d)))
        args.append(final_w.reshape(1, d))
    return pl.pallas_call(
        functools.partial(_post_kernel, final_norm=final_w is not None),
        out_shape=jax.ShapeDtypeStruct((t, d), F32),
        grid=(t // ROW_TILE,),
        in_specs=in_specs,
        out_specs=row(d),
        scratch_shapes=[pltpu.VMEM((ROW_TILE, FFN_HIDDEN), BF16)],
        compiler_params=pltpu.CompilerParams(
            dimension_semantics=("parallel",), vmem_limit_bytes=VMEM_LIMIT),
        name="outproj_ffn",
    )(*args)


def _lambda_init(layer_idx):
    return 0.8 - 0.6 * math.exp(-0.3 * layer_idx)


def kernel(x, positions, norm_mix_w, norm_ffn_w, final_norm_w, attn_w_in, attn_w_out, attn_lambda_q1, attn_lambda_k1, attn_lambda_q2, attn_lambda_k2, attn_subln_w, hgrn_w_in, hgrn_w_out, hgrn_gnorm_w, hgrn_lb_param, ffn_w_in, ffn_w_out):
    b, s, d = x.shape
    depth = norm_mix_w.shape[0]
    x2 = x.reshape(b * s, d)
    rope = _rope_tables(positions)
    for i in range(depth):
        j = i // 2
        last = final_norm_w if i == depth - 1 else None
        if i % 2 == 0:
            qkv = _project(x2, norm_mix_w[i], attn_w_in[j].astype(BF16), rope=rope)
            lam_params = jnp.stack([attn_lambda_q1[j], attn_lambda_k1[j],
                                    attn_lambda_q2[j], attn_lambda_k2[j]]).astype(F32)
            y = _diff_attention(qkv.reshape(b, s, 3 * d), lam_params, attn_subln_w[j], _lambda_init(i))
            w_o = attn_w_out[j]
        else:
            u = _project(x2, norm_mix_w[i], hgrn_w_in[j].astype(BF16))
            y = _hgrn_recurrence(u.reshape(b, s, 4 * d), hgrn_lb_param, hgrn_gnorm_w[j], i)
            w_o = hgrn_w_out[j]
        x2 = _post_mixer(y.reshape(b * s, d), x2, w_o.astype(BF16), norm_ffn_w[i],
                         ffn_w_in[i].astype(BF16), ffn_w_out[i].astype(BF16), last)
    return x2.reshape(b, s, d)
```

```python
import functools
import math

import jax
import jax.numpy as jnp
from jax import lax
from jax.experimental import pallas as pl
from jax.experimental.pallas import tpu as pltpu

F32 = jnp.float32
BF16 = jnp.bfloat16

D_MODEL = 1024
HEADS = 8
HEAD_W = 128
MAP_DIM = 64
ROPE_HALF = MAP_DIM // 2
ROPE_THETA = 10000.0
FFN_HIDDEN = 2816
NORM_EPS = 1e-6
SUBLN_EPS = 1e-5
GNORM_EPS = 1e-6

ROW_TILE = 512
COL_CHUNK = 512
FFN_CHUNK = 256
ATTN_BLOCK = 256
ATTN_UNROLL = 4
HGRN_CHUNK = 128
HGRN_STEP_ROWS = 512
VMEM_LIMIT = 56 * 1024 * 1024
MASK_VALUE = float(jnp.finfo(jnp.float32).min)


def _rms_scale(x, eps):
    return lax.rsqrt(jnp.mean(x * x, axis=-1, keepdims=True) + eps)


def _const_spec(shape):
    return pl.BlockSpec(shape, lambda *_: (0,) * len(shape), pipeline_mode=pl.Buffered(1))


def _rope_table_kernel(pos_ref, invf_ref, cos_ref, sin_ref):
    ang = pos_ref[...].astype(F32) * invf_ref[...]
    lane = lax.broadcasted_iota(jnp.int32, ang.shape, 1)
    first_half = (lane & ROPE_HALF) == 0
    s = jnp.sin(ang)
    cos_ref[...] = jnp.cos(ang)
    sin_ref[...] = jnp.where(first_half, -s, s)


def _rope_tables(positions):
    t = positions.size
    inv_freq = 1.0 / (ROPE_THETA ** (jnp.arange(0, MAP_DIM, 2, dtype=F32) / MAP_DIM))
    invf = jnp.tile(inv_freq, HEAD_W // ROPE_HALF).reshape(1, HEAD_W)
    tm = ROW_TILE
    return pl.pallas_call(
        _rope_table_kernel,
        out_shape=(jax.ShapeDtypeStruct((t, HEAD_W), F32),) * 2,
        grid=(t // tm,),
        in_specs=[pl.BlockSpec((tm, 1), lambda i: (i, 0)), _const_spec((1, HEAD_W))],
        out_specs=(pl.BlockSpec((tm, HEAD_W), lambda i: (i, 0)),) * 2,
        name="rope_tables",
    )(positions.reshape(t, 1), invf)


def _proj_kernel(x_ref, nw_ref, w_ref, *rest, rope_cols, q_cols, q_scale):
    if rope_cols:
        cos_ref, sin_ref, o_ref = rest
    else:
        (o_ref,) = rest
    x = x_ref[...]
    xn = (x * _rms_scale(x, NORM_EPS) * nw_ref[...]).astype(BF16)
    n_out = o_ref.shape[1]
    if rope_cols:
        cos = cos_ref[...]
        sin = sin_ref[...]
        lane = lax.broadcasted_iota(jnp.int32, cos.shape, 1)
        first_half = (lane & ROPE_HALF) == 0
    for c0 in range(0, n_out, COL_CHUNK):
        y = jnp.dot(xn, w_ref[:, c0:c0 + COL_CHUNK], preferred_element_type=F32)
        if c0 < rope_cols:
            for g0 in range(0, COL_CHUNK, HEAD_W):
                yg = y[:, g0:g0 + HEAD_W]
                rot = jnp.where(first_half, pltpu.roll(yg, HEAD_W - ROPE_HALF, 1),
                                pltpu.roll(yg, ROPE_HALF, 1))
                r = yg * cos + rot * sin
                if c0 < q_cols:
                    r = r * q_scale
                o_ref[:, c0 + g0:c0 + g0 + HEAD_W] = r.astype(o_ref.dtype)
        else:
            o_ref[:, c0:c0 + COL_CHUNK] = y.astype(o_ref.dtype)


def _project(x2, norm_w, w, rope=None):
    t, d = x2.shape
    n_out = w.shape[1]
    in_specs = [pl.BlockSpec((ROW_TILE, d), lambda i: (i, 0)),
                _const_spec((1, d)), _const_spec((d, n_out))]
    args = [x2, norm_w.reshape(1, d), w]
    if rope is not None:
        in_specs += [pl.BlockSpec((ROW_TILE, HEAD_W), lambda i: (i, 0))] * 2
        args += list(rope)
    kern = functools.partial(
        _proj_kernel, rope_cols=2 * d if rope is not None else 0,
        q_cols=d, q_scale=MAP_DIM ** -0.5 * math.log2(math.e))
    return pl.pallas_call(
        kern,
        out_shape=jax.ShapeDtypeStruct((t, n_out), BF16),
        grid=(t // ROW_TILE,),
        in_specs=in_specs,
        out_specs=pl.BlockSpec((ROW_TILE, n_out), lambda i: (i, 0)),
        compiler_params=pltpu.CompilerParams(
            dimension_semantics=("parallel",), vmem_limit_bytes=VMEM_LIMIT),
        name="norm_proj_rope" if rope is not None else "norm_proj",
    )(*args)


def _attn_kernel(qi_tab, kj_tab, q_ref, k_ref, v_ref, lam_ref, sw_ref, o_ref,
                 qm_ref, m_ref, acc_ref, s0_ref, s1_ref, p0_ref, p1_ref, a0_ref, a1_ref,
                 *, lambda_init, n_full, n_diag):
    t = ATTN_BLOCK
    n_q = q_ref.shape[0] // t
    nt = (((1,), (1,)), ((), ()))
    s_refs, p_refs, a_refs = (s0_ref, s1_ref), (p0_ref, p1_ref), (a0_ref, a1_ref)

    def split_maps(i, _):
        rows = pl.ds(pl.multiple_of(i * t, t), t)
        q = q_ref[rows, :]
        lane = lax.broadcasted_iota(jnp.int32, q.shape, 1)
        zero = jnp.zeros_like(q)
        qm_ref[0, rows, :] = jnp.where(lane < MAP_DIM, q, zero)
        qm_ref[1, rows, :] = jnp.where(lane >= MAP_DIM, q, zero)
        m_ref[i] = jnp.full(m_ref.shape[1:], -1e30, F32)
        acc_ref[i] = jnp.zeros(acc_ref.shape[1:], F32)
        return 0

    lax.fori_loop(0, n_q, split_maps, 0)
    ones = jnp.ones((t, HEAD_W), BF16)
    tril = (lax.broadcasted_iota(jnp.int32, (t, t), 1)
            <= lax.broadcasted_iota(jnp.int32, (t, t), 0))

    def scores(b, slot):
        rows = pl.ds(pl.multiple_of(qi_tab[b] * t, t), t)
        k = k_ref[pl.ds(pl.multiple_of(kj_tab[b] * t, t), t), :]
        for c in range(2):
            s_refs[slot][c] = lax.dot_general(qm_ref[c, rows, :], k, nt, preferred_element_type=F32)

    def softmax(b, slot, masked):
        qi = qi_tab[b]
        for c in range(2):
            sc = s_refs[slot][c]
            if masked:
                sc = jnp.where(tril, sc, MASK_VALUE)
            halves = [sc[:, g:g + HEAD_W] for g in range(0, t, HEAD_W)]
            m = m_ref[qi, c]
            m_new = jnp.maximum(m, jnp.max(functools.reduce(jnp.maximum, halves),
                                           axis=-1, keepdims=True))
            m_ref[qi, c] = m_new
            a_refs[slot][c] = jnp.exp2(m - m_new)
            p_refs[slot][c] = jnp.concatenate(
                [jnp.exp2(h - m_new).astype(BF16) for h in halves], axis=1)

    def values(b, slot):
        qi = qi_tab[b]
        v_aug = jnp.concatenate([v_ref[pl.ds(pl.multiple_of(kj_tab[b] * t, t), t), :], ones], axis=1)
        for c in range(2):
            pv = jnp.dot(p_refs[slot][c], v_aug, preferred_element_type=F32)
            alpha = a_refs[slot][c]
            for g in range(0, 2 * HEAD_W, HEAD_W):
                acc_ref[qi, c, :, g:g + HEAD_W] = (alpha * acc_ref[qi, c, :, g:g + HEAD_W]
                                                   + pv[:, g:g + HEAD_W])

    def run(first, count, masked):
        if count % 2:
            last = first + count - 1
            scores(last, 0)
            softmax(last, 0, masked)
            values(last, 0)
            count -= 1
        if count == 0:
            return

        def tick(b, parity):
            scores(b, parity)
            softmax(b - 1, 1 - parity, masked)
            values(b - 2, parity)

        scores(first, 0)
        scores(first + 1, 1)
        softmax(first, 0, masked)
        unroll = ATTN_UNROLL
        n_iter = (count - 2) // unroll

        def body(i, _):
            for u in range(unroll):
                tick(first + 2 + unroll * i + u, u % 2)
            return 0

        lax.fori_loop(0, n_iter, body, 0)
        for b in range(first + 2 + unroll * n_iter, first + count):
            tick(b, (b - first) % 2)
        last = first + count - 1
        softmax(last, 1, masked)
        values(last - 1, 0)
        values(last, 1)

    run(0, n_full, False)
    run(n_full, n_diag, True)

    lam_p = lam_ref[...]
    lam = (jnp.exp(jnp.sum(lam_p[0:1] * lam_p[1:2], axis=-1, keepdims=True))
           - jnp.exp(jnp.sum(lam_p[2:3] * lam_p[3:4], axis=-1, keepdims=True)) + lambda_init)
    sw = sw_ref[...] * (1.0 - lambda_init)

    def finish(i, _):
        acc0, acc1 = acc_ref[i, 0], acc_ref[i, 1]
        o = acc0[:, :HEAD_W] / acc0[:, HEAD_W:] - lam * (acc1[:, :HEAD_W] / acc1[:, HEAD_W:])
        o = o * _rms_scale(o, SUBLN_EPS) * sw
        o_ref[pl.ds(pl.multiple_of(i * t, t), t), :] = o.astype(o_ref.dtype)
        return 0

    lax.fori_loop(0, n_q, finish, 0, unroll=2)


def _diff_attention(qkv3, lam_params, subln_w, lambda_init):
    b, s, _ = qkv3.shape
    t = ATTN_BLOCK
    n_q = s // t
    pairs = [(i, j) for i in range(n_q) for j in range(i)] + [(i, i) for i in range(n_q)]
    n_diag, n_full = n_q, len(pairs) - n_q
    qi_tab = jnp.asarray([p[0] for p in pairs], jnp.int32)
    kj_tab = jnp.asarray([p[1] for p in pairs], jnp.int32)
    kern = functools.partial(_attn_kernel, lambda_init=lambda_init, n_full=n_full, n_diag=n_diag)
    head = lambda col0: pl.BlockSpec((None, s, HEAD_W), lambda bi, h, *_: (bi, 0, col0 + h))
    pair_buf = lambda n, dt: [pltpu.VMEM((2, t, n), dt)] * 2
    return pl.pallas_call(
        kern,
        out_shape=jax.ShapeDtypeStruct((b, s, D_MODEL), BF16),
        grid_spec=pltpu.PrefetchScalarGridSpec(
            num_scalar_prefetch=2,
            grid=(b, HEADS),
            in_specs=[head(0), head(HEADS), head(2 * HEADS),
                      _const_spec((4, MAP_DIM)), _const_spec((1, HEAD_W))],
            out_specs=head(0),
            scratch_shapes=[
                pltpu.VMEM((2, s, HEAD_W), BF16),
                pltpu.VMEM((n_q, 2, t, HEAD_W), F32),
                pltpu.VMEM((n_q, 2, t, 2 * HEAD_W), F32),
                *pair_buf(t, F32), *pair_buf(t, BF16), *pair_buf(HEAD_W, F32)],
        ),
        compiler_params=pltpu.CompilerParams(
            dimension_semantics=("parallel", "parallel"), vmem_limit_bytes=VMEM_LIMIT),
        name="diff_flash_attention",
    )(qi_tab, kj_tab, qkv3, qkv3, qkv3, lam_params, subln_w.reshape(1, HEAD_W))


def _split2(x):
    hi = x.astype(BF16)
    return hi, (x - hi.astype(F32)).astype(BF16)


def _silu(x):
    h = 0.5 * x
    return h * jnp.tanh(h) + h


def _hgrn_kernel(q_ref, f_ref, i_ref, g_ref, lbp_ref, gw_ref, o_ref, state_ref, lb_ref, *, layer):
    c = HGRN_CHUNK

    @pl.when(pl.program_id(1) == 0)
    def _():
        state_ref[...] = jnp.zeros_like(state_ref)
        p = lbp_ref[...]
        e = jnp.exp(p - jnp.max(p, axis=0, keepdims=True))
        lb_ref[...] = (jnp.sum(e[1:layer + 1], axis=0, keepdims=True)
                       / jnp.sum(e, axis=0, keepdims=True))

    for r0 in range(0, q_ref.shape[0], c):
        rows = slice(r0, r0 + c)
        _hgrn_chunk(q_ref.at[rows], f_ref.at[rows], i_ref.at[rows], g_ref.at[rows],
                    lb_ref[...], gw_ref[...], o_ref.at[rows], state_ref)


def _hgrn_chunk(q_ref, f_ref, i_ref, g_ref, lb, gw, o_ref, state_ref):
    c = HGRN_CHUNK
    sig = 1.0 / (1.0 + jnp.exp(-f_ref[...].astype(F32)))
    f = lb + (1.0 - lb) * sig
    kk = 1.0 - f
    logf = jnp.log2(f)
    r_idx = lax.broadcasted_iota(jnp.int32, (c, c), 0)
    c_idx = lax.broadcasted_iota(jnp.int32, (c, c), 1)
    causal = c_idx <= r_idx
    tri = causal.astype(BF16)
    cum = sum(jnp.dot(tri, part, preferred_element_type=F32) for part in _split2(logf))
    cum_mid = cum[c // 2 - 1:c // 2]
    cum_last = cum[c - 1:c]
    q_mid = _silu(q_ref[...].astype(F32)) * jnp.exp2(cum - cum_mid)
    k_mid = kk * jnp.exp2(cum_mid - cum)
    q_intra = q_mid.astype(BF16)
    k_intra = k_mid.astype(BF16)
    q_state = (q_mid * jnp.exp2(cum_mid)).astype(BF16)
    k_end = (k_mid * jnp.exp2(cum_last - cum_mid)).astype(BF16)
    chunk_decay = jnp.exp2(cum_last)
    v = i_ref[...]
    g = _silu(g_ref[...].astype(F32))
    nt = (((1,), (1,)), ((), ()))
    tn = (((0,), (0,)), ((), ()))
    for h in range(HEADS):
        sl = slice(h * HEAD_W, (h + 1) * HEAD_W)
        st = state_ref[h]
        a = lax.dot_general(q_intra[:, sl], k_intra[:, sl], nt, preferred_element_type=F32)
        a = jnp.where(causal, a, 0.0).astype(BF16)
        o = (jnp.dot(a, v[:, sl], preferred_element_type=F32)
             + lax.dot_general(q_state[:, sl], st.astype(BF16), nt, preferred_element_type=F32))
        state_ref[h] = chunk_decay[:, sl] * st + lax.dot_general(
            v[:, sl], k_end[:, sl], tn, preferred_element_type=F32)
        o = o * _rms_scale(o, GNORM_EPS) * gw * g[:, sl]
        o_ref[:, sl] = o.astype(o_ref.dtype)


def _hgrn_recurrence(u3, lb_param, gnorm_w, layer):
    b, s, _ = u3.shape
    c = HGRN_STEP_ROWS
    sec = lambda k: pl.BlockSpec((None, c, D_MODEL), lambda bi, n: (bi, n, k))
    return pl.pallas_call(
        functools.partial(_hgrn_kernel, layer=layer),
        out_shape=jax.ShapeDtypeStruct((b, s, D_MODEL), BF16),
        grid=(b, s // c),
        in_specs=[sec(0), sec(1), sec(2), sec(3), _const_spec(lb_param.shape),
                  _const_spec((1, HEAD_W))],
        out_specs=pl.BlockSpec((None, c, D_MODEL), lambda bi, n: (bi, n, 0)),
        scratch_shapes=[pltpu.VMEM((HEADS, HEAD_W, HEAD_W), F32), pltpu.VMEM((1, D_MODEL), F32)],
        compiler_params=pltpu.CompilerParams(
            dimension_semantics=("parallel", "arbitrary"), vmem_limit_bytes=VMEM_LIMIT),
        name="hgrn2_recurrence",
    )(u3, u3, u3, u3, lb_param.astype(F32), gnorm_w.reshape(1, HEAD_W))


def _post_kernel(y_ref, x_ref, wo_ref, nw_ref, win_ref, wout_ref, *rest, final_norm):
    if final_norm:
        fw_ref, o_ref, h_ref = rest
    else:
        o_ref, h_ref = rest
    x = x_ref[...] + jnp.dot(y_ref[...], wo_ref[...], preferred_element_type=F32)
    xn = (x * _rms_scale(x, NORM_EPS) * nw_ref[...]).astype(BF16)
    for c0 in range(0, FFN_HIDDEN, FFN_CHUNK):
        gate = jnp.dot(xn, win_ref[:, c0:c0 + FFN_CHUNK], preferred_element_type=F32)
        up = jnp.dot(xn, win_ref[:, FFN_HIDDEN + c0:FFN_HIDDEN + c0 + FFN_CHUNK],
                     preferred_element_type=F32)
        h_ref[:, c0:c0 + FFN_CHUNK] = (_silu(gate) * up).astype(BF16)
    x = x + jnp.dot(h_ref[...], wout_ref[...], preferred_element_type=F32)
    if final_norm:
        x = x * _rms_scale(x, NORM_EPS) * fw_ref[...]
    o_ref[...] = x


def _post_mixer(y2, x2, w_o, norm_w, w_in, w_out, final_w=None):
    t, d = x2.shape
    row = lambda n: pl.BlockSpec((ROW_TILE, n), lambda i: (i, 0))
    in_specs = [row(d), row(d), _const_spec((d, d)), _const_spec((1, d)),
                _const_spec((d, 2 * FFN_HIDDEN)), _const_spec((FFN_HIDDEN, d))]
    args = [y2, x2, w_o, norm_w.reshape(1, d), w_in, w_out]
    if final_w is not None:
        in_specs.append(_const_spec((1, d)))
        args.append(final_w.reshape(1, d))
    return pl.pallas_call(
        functools.partial(_post_kernel, final_norm=final_w is not None),
        out_shape=jax.ShapeDtypeStruct((t, d), F32),
        grid=(t // ROW_TILE,),
        in_specs=in_specs,
        out_specs=row(d),
        scratch_shapes=[pltpu.VMEM((ROW_TILE, FFN_HIDDEN), BF16)],
        compiler_params=pltpu.CompilerParams(
            dimension_semantics=("parallel",), vmem_limit_bytes=VMEM_LIMIT),
        name="outproj_ffn",
    )(*args)


def _lambda_init(layer_idx):
    return 0.8 - 0.6 * math.exp(-0.3 * layer_idx)


def kernel(x, positions, norm_mix_w, norm_ffn_w, final_norm_w, attn_w_in, attn_w_out, attn_lambda_q1, attn_lambda_k1, attn_lambda_q2, attn_lambda_k2, attn_subln_w, hgrn_w_in, hgrn_w_out, hgrn_gnorm_w, hgrn_lb_param, ffn_w_in, ffn_w_out):
    b, s, d = x.shape
    depth = norm_mix_w.shape[0]
    x2 = x.reshape(b * s, d)
    rope = _rope_tables(positions)
    for i in range(depth):
        j = i // 2
        last = final_norm_w if i == depth - 1 else None
        if i % 2 == 0:
            qkv = _project(x2, norm_mix_w[i], attn_w_in[j].astype(BF16), rope=rope)
            lam_params = jnp.stack([attn_lambda_q1[j], attn_lambda_k1[j],
                                    attn_lambda_q2[j], attn_lambda_k2[j]]).astype(F32)
            y = _diff_attention(qkv.reshape(b, s, 3 * d), lam_params, attn_subln_w[j], _lambda_init(i))
            w_o = attn_w_out[j]
        else:
            u = _project(x2, norm_mix_w[i], hgrn_w_in[j].astype(BF16))
            y = _hgrn_recurrence(u.reshape(b, s, 4 * d), hgrn_lb_param, hgrn_gnorm_w[j], i)
            w_o = hgrn_w_out[j]
        x2 = _post_mixer(y.reshape(b * s, d), x2, w_o.astype(BF16), norm_ffn_w[i],
                         ffn_w_in[i].astype(BF16), ffn_w_out[i].astype(BF16), last)
    return x2.reshape(b, s, d)
```

```python
import functools
import math

import jax
import jax.numpy as jnp
from jax import lax
from jax.experimental import pallas as pl
from jax.experimental.pallas import tpu as pltpu

F32 = jnp.float32
BF16 = jnp.bfloat16

D_MODEL = 1024
HEADS = 8
HEAD_W = 128
MAP_DIM = 64
ROPE_HALF = MAP_DIM // 2
ROPE_THETA = 10000.0
FFN_HIDDEN = 2816
NORM_EPS = 1e-6
SUBLN_EPS = 1e-5
GNORM_EPS = 1e-6

ROW_TILE = 512
COL_CHUNK = 512
FFN_CHUNK = 256
ATTN_BLOCK = 256
ATTN_UNROLL = 8
HGRN_CHUNK = 128
HGRN_STEP_ROWS = 512
VMEM_LIMIT = 56 * 1024 * 1024
MASK_VALUE = float(jnp.finfo(jnp.float32).min)


def _rms_scale(x, eps):
    return lax.rsqrt(jnp.mean(x * x, axis=-1, keepdims=True) + eps)


def _const_spec(shape):
    return pl.BlockSpec(shape, lambda *_: (0,) * len(shape), pipeline_mode=pl.Buffered(1))


def _rope_table_kernel(pos_ref, invf_ref, cos_ref, sin_ref):
    ang = pos_ref[...].astype(F32) * invf_ref[...]
    lane = lax.broadcasted_iota(jnp.int32, ang.shape, 1)
    first_half = (lane & ROPE_HALF) == 0
    s = jnp.sin(ang)
    cos_ref[...] = jnp.cos(ang)
    sin_ref[...] = jnp.where(first_half, -s, s)


def _rope_tables(positions):
    t = positions.size
    inv_freq = 1.0 / (ROPE_THETA ** (jnp.arange(0, MAP_DIM, 2, dtype=F32) / MAP_DIM))
    invf = jnp.tile(inv_freq, HEAD_W // ROPE_HALF).reshape(1, HEAD_W)
    tm = ROW_TILE
    return pl.pallas_call(
        _rope_table_kernel,
        out_shape=(jax.ShapeDtypeStruct((t, HEAD_W), F32),) * 2,
        grid=(t // tm,),
        in_specs=[pl.BlockSpec((tm, 1), lambda i: (i, 0)), _const_spec((1, HEAD_W))],
        out_specs=(pl.BlockSpec((tm, HEAD_W), lambda i: (i, 0)),) * 2,
        name="rope_tables",
    )(positions.reshape(t, 1), invf)


def _proj_kernel(x_ref, nw_ref, w_ref, *rest, rope_cols, q_cols, q_scale):
    if rope_cols:
        cos_ref, sin_ref, o_ref = rest
    else:
        (o_ref,) = rest
    x = x_ref[...]
    xn = (x * _rms_scale(x, NORM_EPS) * nw_ref[...]).astype(BF16)
    n_out = o_ref.shape[1]
    if rope_cols:
        cos = cos_ref[...]
        sin = sin_ref[...]
        lane = lax.broadcasted_iota(jnp.int32, cos.shape, 1)
        first_half = (lane & ROPE_HALF) == 0
    for c0 in range(0, n_out, COL_CHUNK):
        y = jnp.dot(xn, w_ref[:, c0:c0 + COL_CHUNK], preferred_element_type=F32)
        if c0 < rope_cols:
            for g0 in range(0, COL_CHUNK, HEAD_W):
                yg = y[:, g0:g0 + HEAD_W]
                rot = jnp.where(first_half, pltpu.roll(yg, HEAD_W - ROPE_HALF, 1),
                                pltpu.roll(yg, ROPE_HALF, 1))
                r = yg * cos + rot * sin
                if c0 < q_cols:
                    r = r * q_scale
                o_ref[:, c0 + g0:c0 + g0 + HEAD_W] = r.astype(o_ref.dtype)
        else:
            o_ref[:, c0:c0 + COL_CHUNK] = y.astype(o_ref.dtype)


def _project(x2, norm_w, w, rope=None):
    t, d = x2.shape
    n_out = w.shape[1]
    in_specs = [pl.BlockSpec((ROW_TILE, d), lambda i: (i, 0)),
                _const_spec((1, d)), _const_spec((d, n_out))]
    args = [x2, norm_w.reshape(1, d), w]
    if rope is not None:
        in_specs += [pl.BlockSpec((ROW_TILE, HEAD_W), lambda i: (i, 0))] * 2
        args += list(rope)
    kern = functools.partial(
        _proj_kernel, rope_cols=2 * d if rope is not None else 0,
        q_cols=d, q_scale=MAP_DIM ** -0.5 * math.log2(math.e))
    return pl.pallas_call(
        kern,
        out_shape=jax.ShapeDtypeStruct((t, n_out), BF16),
        grid=(t // ROW_TILE,),
        in_specs=in_specs,
        out_specs=pl.BlockSpec((ROW_TILE, n_out), lambda i: (i, 0)),
        compiler_params=pltpu.CompilerParams(
            dimension_semantics=("parallel",), vmem_limit_bytes=VMEM_LIMIT),
        name="norm_proj_rope" if rope is not None else "norm_proj",
    )(*args)


def _attn_kernel(qi_tab, kj_tab, q_ref, k_ref, v_ref, lam_ref, sw_ref, o_ref,
                 qm_ref, m_ref, acc_ref, s0_ref, s1_ref, p0_ref, p1_ref, a0_ref, a1_ref,
                 *, lambda_init, n_full, n_diag):
    t = ATTN_BLOCK
    n_q = q_ref.shape[0] // t
    nt = (((1,), (1,)), ((), ()))
    s_refs, p_refs, a_refs = (s0_ref, s1_ref), (p0_ref, p1_ref), (a0_ref, a1_ref)

    def split_maps(i, _):
        q = q_ref[pl.ds(pl.multiple_of(i * t, t), t), :]
        lane = lax.broadcasted_iota(jnp.int32, q.shape, 1)
        zero = jnp.zeros_like(q)
        qm_ref[i, :t, :] = jnp.where(lane < MAP_DIM, q, zero)
        qm_ref[i, t:, :] = jnp.where(lane >= MAP_DIM, q, zero)
        m_ref[i] = jnp.full(m_ref.shape[1:], -1e30, F32)
        acc_ref[i] = jnp.zeros(acc_ref.shape[1:], F32)
        return 0

    lax.fori_loop(0, n_q, split_maps, 0)
    ones = jnp.ones((t, HEAD_W), BF16)
    tril = (lax.broadcasted_iota(jnp.int32, (2 * t, t), 1)
            <= (lax.broadcasted_iota(jnp.int32, (2 * t, t), 0) & (t - 1)))

    def scores(b, slot):
        k = k_ref[pl.ds(pl.multiple_of(kj_tab[b] * t, t), t), :]
        s_refs[slot][...] = lax.dot_general(qm_ref[qi_tab[b]], k, nt, preferred_element_type=F32)

    def softmax(b, slot, masked):
        qi = qi_tab[b]
        sc = s_refs[slot][...]
        if masked:
            sc = jnp.where(tril, sc, MASK_VALUE)
        halves = [sc[:, g:g + HEAD_W] for g in range(0, t, HEAD_W)]
        m = m_ref[qi]
        m_new = jnp.maximum(m, jnp.max(functools.reduce(jnp.maximum, halves),
                                       axis=-1, keepdims=True))
        m_ref[qi] = m_new
        a_refs[slot][...] = jnp.exp2(m - m_new)
        p_refs[slot][...] = jnp.concatenate(
            [jnp.exp2(h - m_new).astype(BF16) for h in halves], axis=1)

    def values(b, slot):
        qi = qi_tab[b]
        v_aug = jnp.concatenate([v_ref[pl.ds(pl.multiple_of(kj_tab[b] * t, t), t), :], ones], axis=1)
        pv = jnp.dot(p_refs[slot][...], v_aug, preferred_element_type=F32)
        alpha = a_refs[slot][...]
        for g in range(0, 2 * HEAD_W, HEAD_W):
            acc_ref[qi, :, g:g + HEAD_W] = alpha * acc_ref[qi, :, g:g + HEAD_W] + pv[:, g:g + HEAD_W]

    def run(first, count, masked):
        if count % 2:
            last = first + count - 1
            scores(last, 0)
            softmax(last, 0, masked)
            values(last, 0)
            count -= 1
        if count == 0:
            return

        def tick(b, parity):
            scores(b, parity)
            softmax(b - 1, 1 - parity, masked)
            values(b - 2, parity)

        scores(first, 0)
        scores(first + 1, 1)
        softmax(first, 0, masked)
        unroll = ATTN_UNROLL
        while unroll > 2 and count - 2 < 2 * unroll:
            unroll //= 2
        n_iter = (count - 2) // unroll

        def body(i, _):
            for u in range(unroll):
                tick(first + 2 + unroll * i + u, u % 2)
            return 0

        lax.fori_loop(0, n_iter, body, 0)
        for b in range(first + 2 + unroll * n_iter, first + count):
            tick(b, (b - first) % 2)
        last = first + count - 1
        softmax(last, 1, masked)
        values(last - 1, 0)
        values(last, 1)

    run(0, n_full, False)
    run(n_full, n_diag, True)

    lam_p = lam_ref[...]
    lam = (jnp.exp(jnp.sum(lam_p[0:1] * lam_p[1:2], axis=-1, keepdims=True))
           - jnp.exp(jnp.sum(lam_p[2:3] * lam_p[3:4], axis=-1, keepdims=True)) + lambda_init)
    sw = sw_ref[...] * (1.0 - lambda_init)

    def finish(i, _):
        acc0, acc1 = acc_ref[i, :t], acc_ref[i, t:]
        o = acc0[:, :HEAD_W] / acc0[:, HEAD_W:] - lam * (acc1[:, :HEAD_W] / acc1[:, HEAD_W:])
        o = o * _rms_scale(o, SUBLN_EPS) * sw
        o_ref[pl.ds(pl.multiple_of(i * t, t), t), :] = o.astype(o_ref.dtype)
        return 0

    lax.fori_loop(0, n_q, finish, 0, unroll=2)


def _diff_attention(qkv3, lam_params, subln_w, lambda_init):
    b, s, _ = qkv3.shape
    t = ATTN_BLOCK
    n_q = s // t
    pairs = [(i, j) for i in range(n_q) for j in range(i)] + [(i, i) for i in range(n_q)]
    n_diag, n_full = n_q, len(pairs) - n_q
    qi_tab = jnp.asarray([p[0] for p in pairs], jnp.int32)
    kj_tab = jnp.asarray([p[1] for p in pairs], jnp.int32)
    kern = functools.partial(_attn_kernel, lambda_init=lambda_init, n_full=n_full, n_diag=n_diag)
    head = lambda col0: pl.BlockSpec((None, s, HEAD_W), lambda bi, h, *_: (bi, 0, col0 + h))
    pair_buf = lambda n, dt: [pltpu.VMEM((2 * t, n), dt)] * 2
    return pl.pallas_call(
        kern,
        out_shape=jax.ShapeDtypeStruct((b, s, D_MODEL), BF16),
        grid_spec=pltpu.PrefetchScalarGridSpec(
            num_scalar_prefetch=2,
            grid=(b, HEADS),
            in_specs=[head(0), head(HEADS), head(2 * HEADS),
                      _const_spec((4, MAP_DIM)), _const_spec((1, HEAD_W))],
            out_specs=head(0),
            scratch_shapes=[
                pltpu.VMEM((n_q, 2 * t, HEAD_W), BF16),
                pltpu.VMEM((n_q, 2 * t, HEAD_W), F32),
                pltpu.VMEM((n_q, 2 * t, 2 * HEAD_W), F32),
                *pair_buf(t, F32), *pair_buf(t, BF16), *pair_buf(HEAD_W, F32)],
        ),
        compiler_params=pltpu.CompilerParams(
            dimension_semantics=("parallel", "parallel"), vmem_limit_bytes=VMEM_LIMIT),
        name="diff_flash_attention",
    )(qi_tab, kj_tab, qkv3, qkv3, qkv3, lam_params, subln_w.reshape(1, HEAD_W))


def _split2(x):
    hi = x.astype(BF16)
    return hi, (x - hi.astype(F32)).astype(BF16)


def _silu(x):
    h = 0.5 * x
    return h * jnp.tanh(h) + h


def _hgrn_kernel(q_ref, f_ref, i_ref, g_ref, lbp_ref, gw_ref, o_ref, state_ref, lb_ref, *, layer):
    c = HGRN_CHUNK

    @pl.when(pl.program_id(1) == 0)
    def _():
        state_ref[...] = jnp.zeros_like(state_ref)
        p = lbp_ref[...]
        e = jnp.exp(p - jnp.max(p, axis=0, keepdims=True))
        lb_ref[...] = (jnp.sum(e[1:layer + 1], axis=0, keepdims=True)
                       / jnp.sum(e, axis=0, keepdims=True))

    for r0 in range(0, q_ref.shape[0], c):
        rows = slice(r0, r0 + c)
        _hgrn_chunk(q_ref.at[rows], f_ref.at[rows], i_ref.at[rows], g_ref.at[rows],
                    lb_ref[...], gw_ref[...], o_ref.at[rows], state_ref)


def _hgrn_chunk(q_ref, f_ref, i_ref, g_ref, lb, gw, o_ref, state_ref):
    c = HGRN_CHUNK
    sig = 1.0 / (1.0 + jnp.exp(-f_ref[...].astype(F32)))
    f = lb + (1.0 - lb) * sig
    kk = 1.0 - f
    logf = jnp.log2(f)
    r_idx = lax.broadcasted_iota(jnp.int32, (c, c), 0)
    c_idx = lax.broadcasted_iota(jnp.int32, (c, c), 1)
    causal = c_idx <= r_idx
    tri = causal.astype(BF16)
    cum = sum(jnp.dot(tri, part, preferred_element_type=F32) for part in _split2(logf))
    cum_mid = cum[c // 2 - 1:c // 2]
    cum_last = cum[c - 1:c]
    q_mid = _silu(q_ref[...].astype(F32)) * jnp.exp2(cum - cum_mid)
    k_mid = kk * jnp.exp2(cum_mid - cum)
    q_intra = q_mid.astype(BF16)
    k_intra = k_mid.astype(BF16)
    q_state = (q_mid * jnp.exp2(cum_mid)).astype(BF16)
    k_end = (k_mid * jnp.exp2(cum_last - cum_mid)).astype(BF16)
    chunk_decay = jnp.exp2(cum_last)
    v = i_ref[...]
    g = _silu(g_ref[...].astype(F32))
    nt = (((1,), (1,)), ((), ()))
    tn = (((0,), (0,)), ((), ()))
    for h in range(HEADS):
        sl = slice(h * HEAD_W, (h + 1) * HEAD_W)
        st = state_ref[h]
        a = lax.dot_general(q_intra[:, sl], k_intra[:, sl], nt, preferred_element_type=F32)
        a = jnp.where(causal, a, 0.0).astype(BF16)
        o = (jnp.dot(a, v[:, sl], preferred_element_type=F32)
             + lax.dot_general(q_state[:, sl], st.astype(BF16), nt, preferred_element_type=F32))
        state_ref[h] = chunk_decay[:, sl] * st + lax.dot_general(
            v[:, sl], k_end[:, sl], tn, preferred_element_type=F32)
        o = o * _rms_scale(o, GNORM_EPS) * gw * g[:, sl]
        o_ref[:, sl] = o.astype(o_ref.dtype)


def _hgrn_recurrence(u3, lb_param, gnorm_w, layer):
    b, s, _ = u3.shape
    c = HGRN_STEP_ROWS
    sec = lambda k: pl.BlockSpec((None, c, D_MODEL), lambda bi, n: (bi, n, k))
    return pl.pallas_call(
        functools.partial(_hgrn_kernel, layer=layer),
        out_shape=jax.ShapeDtypeStruct((b, s, D_MODEL), BF16),
        grid=(b, s // c),
        in_specs=[sec(0), sec(1), sec(2), sec(3), _const_spec(lb_param.shape),
                  _const_spec((1, HEAD_W))],
        out_specs=pl.BlockSpec((None, c, D_MODEL), lambda bi, n: (bi, n, 0)),
        scratch_shapes=[pltpu.VMEM((HEADS, HEAD_W, HEAD_W), F32), pltpu.VMEM((1, D_MODEL), F32)],
        compiler_params=pltpu.CompilerParams(
            dimension_semantics=("parallel", "arbitrary"), vmem_limit_bytes=VMEM_LIMIT),
        name="hgrn2_recurrence",
    )(u3, u3, u3, u3, lb_param.astype(F32), gnorm_w.reshape(1, HEAD_W))


def _post_kernel(y_ref, x_ref, wo_ref, nw_ref, win_ref, wout_ref, *rest, final_norm):
    if final_norm:
        fw_ref, o_ref, h_ref = rest
    else:
        o_ref, h_ref = rest
    x = x_ref[...] + jnp.dot(y_ref[...], wo_ref[...], preferred_element_type=F32)
    xn = (x * _rms_scale(x, NORM_EPS) * nw_ref[...]).astype(BF16)
    for c0 in range(0, FFN_HIDDEN, FFN_CHUNK):
        gate = jnp.dot(xn, win_ref[:, c0:c0 + FFN_CHUNK], preferred_element_type=F32)
        up = jnp.dot(xn, win_ref[:, FFN_HIDDEN + c0:FFN_HIDDEN + c0 + FFN_CHUNK],
                     preferred_element_type=F32)
        h_ref[:, c0:c0 + FFN_CHUNK] = (_silu(gate) * up).astype(BF16)
    x = x + jnp.dot(h_ref[...], wout_ref[...], preferred_element_type=F32)
    if final_norm:
        x = x * _rms_scale(x, NORM_EPS) * fw_ref[...]
    o_ref[...] = x


def _post_mixer(y2, x2, w_o, norm_w, w_in, w_out, final_w=None):
    t, d = x2.shape
    row = lambda n: pl.BlockSpec((ROW_TILE, n), lambda i: (i, 0))
    in_specs = [row(d), row(d), _const_spec((d, d)), _const_spec((1, d)),
                _const_spec((d, 2 * FFN_HIDDEN)), _const_spec((FFN_HIDDEN, d))]
    args = [y2, x2, w_o, norm_w.reshape(1, d), w_in, w_out]
    if final_w is not None:
        in_specs.append(_const_spec((1, d)))
        args.append(final_w.reshape(1, d))
    return pl.pallas_call(
        functools.partial(_post_kernel, final_norm=final_w is not None),
        out_shape=jax.ShapeDtypeStruct((t, d), F32),
        grid=(t // ROW_TILE,),
        in_specs=in_specs,
        out_specs=row(d),
        scratch_shapes=[pltpu.VMEM((ROW_TILE, FFN_HIDDEN), BF16)],
        compiler_params=pltpu.CompilerParams(
            dimension_semantics=("parallel",), vmem_limit_bytes=VMEM_LIMIT),
        name="outproj_ffn",
    )(*args)


def _lambda_init(layer_idx):
    return 0.8 - 0.6 * math.exp(-0.3 * layer_idx)


def kernel(x, positions, norm_mix_w, norm_ffn_w, final_norm_w, attn_w_in, attn_w_out, attn_lambda_q1, attn_lambda_k1, attn_lambda_q2, attn_lambda_k2, attn_subln_w, hgrn_w_in, hgrn_w_out, hgrn_gnorm_w, hgrn_lb_param, ffn_w_in, ffn_w_out):
    b, s, d = x.shape
    depth = norm_mix_w.shape[0]
    x2 = x.reshape(b * s, d)
    rope = _rope_tables(positions)
    for i in range(depth):
        j = i // 2
        last = final_norm_w if i == depth - 1 else None
        if i % 2 == 0:
            qkv = _project(x2, norm_mix_w[i], attn_w_in[j].astype(BF16), rope=rope)
            lam_params = jnp.stack([attn_lambda_q1[j], attn_lambda_k1[j],
                                    attn_lambda_q2[j], attn_lambda_k2[j]]).astype(F32)
            y = _diff_attention(qkv.reshape(b, s, 3 * d), lam_params, attn_subln_w[j], _lambda_init(i))
            w_o = attn_w_out[j]
        else:
            u = _project(x2, norm_mix_w[i], hgrn_w_in[j].astype(BF16))
            y = _hgrn_recurrence(u.reshape(b, s, 4 * d), hgrn_lb_param, hgrn_gnorm_w[j], i)
            w_o = hgrn_w_out[j]
        x2 = _post_mixer(y.reshape(b * s, d), x2, w_o.astype(BF16), norm_ffn_w[i],
                         ffn_w_in[i].astype(BF16), ffn_w_out[i].astype(BF16), last)
    return x2.reshape(b, s, d)
```

```python
import functools
import math

import jax
import jax.numpy as jnp
from jax import lax
from jax.experimental import pallas as pl
from jax.experimental.pallas import tpu as pltpu

F32 = jnp.float32
BF16 = jnp.bfloat16

D_MODEL = 1024
HEADS = 8
HEAD_W = 128
MAP_DIM = 64
ROPE_HALF = MAP_DIM // 2
ROPE_THETA = 10000.0
FFN_HIDDEN = 2816
NORM_EPS = 1e-6
SUBLN_EPS = 1e-5
GNORM_EPS = 1e-6

ROW_TILE = 512
COL_CHUNK = 512
FFN_CHUNK = 256
ATTN_BLOCK = 256
ATTN_UNROLL = 8
HGRN_CHUNK = 128
HGRN_STEP_ROWS = 1024
VMEM_LIMIT = 56 * 1024 * 1024
MASK_VALUE = float(jnp.finfo(jnp.float32).min)


def _rms_scale(x, eps):
    return lax.rsqrt(jnp.mean(x * x, axis=-1, keepdims=True) + eps)


def _const_spec(shape):
    return pl.BlockSpec(shape, lambda *_: (0,) * len(shape), pipeline_mode=pl.Buffered(1))


def _rope_table_kernel(pos_ref, invf_ref, cos_ref, sin_ref):
    ang = pos_ref[...].astype(F32) * invf_ref[...]
    lane = lax.broadcasted_iota(jnp.int32, ang.shape, 1)
    first_half = (lane & ROPE_HALF) == 0
    s = jnp.sin(ang)
    cos_ref[...] = jnp.cos(ang)
    sin_ref[...] = jnp.where(first_half, -s, s)


def _rope_tables(positions):
    t = positions.size
    inv_freq = 1.0 / (ROPE_THETA ** (jnp.arange(0, MAP_DIM, 2, dtype=F32) / MAP_DIM))
    invf = jnp.tile(inv_freq, HEAD_W // ROPE_HALF).reshape(1, HEAD_W)
    tm = ROW_TILE
    return pl.pallas_call(
        _rope_table_kernel,
        out_shape=(jax.ShapeDtypeStruct((t, HEAD_W), F32),) * 2,
        grid=(t // tm,),
        in_specs=[pl.BlockSpec((tm, 1), lambda i: (i, 0)), _const_spec((1, HEAD_W))],
        out_specs=(pl.BlockSpec((tm, HEAD_W), lambda i: (i, 0)),) * 2,
        name="rope_tables",
    )(positions.reshape(t, 1), invf)


def _proj_kernel(x_ref, nw_ref, w_ref, *rest, rope_cols, q_cols, q_scale):
    if rope_cols:
        cos_ref, sin_ref, o_ref = rest
    else:
        (o_ref,) = rest
    x = x_ref[...]
    xn = (x * _rms_scale(x, NORM_EPS) * nw_ref[...]).astype(BF16)
    n_out = o_ref.shape[1]
    if rope_cols:
        cos = cos_ref[...]
        sin = sin_ref[...]
        lane = lax.broadcasted_iota(jnp.int32, cos.shape, 1)
        first_half = (lane & ROPE_HALF) == 0
    for c0 in range(0, n_out, COL_CHUNK):
        y = jnp.dot(xn, w_ref[:, c0:c0 + COL_CHUNK], preferred_element_type=F32)
        if c0 < rope_cols:
            for g0 in range(0, COL_CHUNK, HEAD_W):
                yg = y[:, g0:g0 + HEAD_W]
                rot = jnp.where(first_half, pltpu.roll(yg, HEAD_W - ROPE_HALF, 1),
                                pltpu.roll(yg, ROPE_HALF, 1))
                r = yg * cos + rot * sin
                if c0 < q_cols:
                    r = r * q_scale
                o_ref[:, c0 + g0:c0 + g0 + HEAD_W] = r.astype(o_ref.dtype)
        else:
            o_ref[:, c0:c0 + COL_CHUNK] = y.astype(o_ref.dtype)


def _project(x2, norm_w, w, rope=None):
    t, d = x2.shape
    n_out = w.shape[1]
    in_specs = [pl.BlockSpec((ROW_TILE, d), lambda i: (i, 0)),
                _const_spec((1, d)), _const_spec((d, n_out))]
    args = [x2, norm_w.reshape(1, d), w]
    if rope is not None:
        in_specs += [pl.BlockSpec((ROW_TILE, HEAD_W), lambda i: (i, 0))] * 2
        args += list(rope)
    kern = functools.partial(
        _proj_kernel, rope_cols=2 * d if rope is not None else 0,
        q_cols=d, q_scale=MAP_DIM ** -0.5 * math.log2(math.e))
    return pl.pallas_call(
        kern,
        out_shape=jax.ShapeDtypeStruct((t, n_out), BF16),
        grid=(t // ROW_TILE,),
        in_specs=in_specs,
        out_specs=pl.BlockSpec((ROW_TILE, n_out), lambda i: (i, 0)),
        compiler_params=pltpu.CompilerParams(
            dimension_semantics=("parallel",), vmem_limit_bytes=VMEM_LIMIT),
        name="norm_proj_rope" if rope is not None else "norm_proj",
    )(*args)


def _attn_kernel(qi_tab, kj_tab, q_ref, k_ref, v_ref, lam_ref, sw_ref, o_ref,
                 qm_ref, m_ref, acc_ref, s0_ref, s1_ref, p0_ref, p1_ref, a0_ref, a1_ref,
                 *, lambda_init, n_full, n_diag):
    t = ATTN_BLOCK
    n_q = q_ref.shape[0] // t
    nt = (((1,), (1,)), ((), ()))
    s_refs, p_refs, a_refs = (s0_ref, s1_ref), (p0_ref, p1_ref), (a0_ref, a1_ref)

    def split_maps(i, _):
        q = q_ref[pl.ds(pl.multiple_of(i * t, t), t), :]
        lane = lax.broadcasted_iota(jnp.int32, q.shape, 1)
        zero = jnp.zeros_like(q)
        qm_ref[i, :t, :] = jnp.where(lane < MAP_DIM, q, zero)
        qm_ref[i, t:, :] = jnp.where(lane >= MAP_DIM, q, zero)
        m_ref[i] = jnp.full(m_ref.shape[1:], -1e30, F32)
        acc_ref[i] = jnp.zeros(acc_ref.shape[1:], F32)
        return 0

    lax.fori_loop(0, n_q, split_maps, 0)
    ones = jnp.ones((t, HEAD_W), BF16)
    tril = (lax.broadcasted_iota(jnp.int32, (2 * t, t), 1)
            <= (lax.broadcasted_iota(jnp.int32, (2 * t, t), 0) & (t - 1)))

    def scores(b, slot):
        k = k_ref[pl.ds(pl.multiple_of(kj_tab[b] * t, t), t), :]
        s_refs[slot][...] = lax.dot_general(qm_ref[qi_tab[b]], k, nt, preferred_element_type=F32)

    def softmax(b, slot, masked):
        qi = qi_tab[b]
        sc = s_refs[slot][...]
        if masked:
            sc = jnp.where(tril, sc, MASK_VALUE)
        halves = [sc[:, g:g + HEAD_W] for g in range(0, t, HEAD_W)]
        m = m_ref[qi]
        m_new = jnp.maximum(m, jnp.max(functools.reduce(jnp.maximum, halves),
                                       axis=-1, keepdims=True))
        m_ref[qi] = m_new
        a_refs[slot][...] = jnp.exp2(m - m_new)
        p_refs[slot][...] = jnp.concatenate(
            [jnp.exp2(h - m_new).astype(BF16) for h in halves], axis=1)

    def values(b, slot):
        qi = qi_tab[b]
        v_aug = jnp.concatenate([v_ref[pl.ds(pl.multiple_of(kj_tab[b] * t, t), t), :], ones], axis=1)
        pv = jnp.dot(p_refs[slot][...], v_aug, preferred_element_type=F32)
        alpha = a_refs[slot][...]
        for g in range(0, 2 * HEAD_W, HEAD_W):
            acc_ref[qi, :, g:g + HEAD_W] = alpha * acc_ref[qi, :, g:g + HEAD_W] + pv[:, g:g + HEAD_W]

    def run(first, count, masked):
        if count % 2:
            last = first + count - 1
            scores(last, 0)
            softmax(last, 0, masked)
            values(last, 0)
            count -= 1
        if count == 0:
            return

        def tick(b, parity):
            scores(b, parity)
            softmax(b - 1, 1 - parity, masked)
            values(b - 2, parity)

        scores(first, 0)
        scores(first + 1, 1)
        softmax(first, 0, masked)
        unroll = ATTN_UNROLL
        while unroll > 2 and count - 2 < 2 * unroll:
            unroll //= 2
        n_iter = (count - 2) // unroll

        def body(i, _):
            for u in range(unroll):
                tick(first + 2 + unroll * i + u, u % 2)
            return 0

        lax.fori_loop(0, n_iter, body, 0)
        for b in range(first + 2 + unroll * n_iter, first + count):
            tick(b, (b - first) % 2)
        last = first + count - 1
        softmax(last, 1, masked)
        values(last - 1, 0)
        values(last, 1)

    run(0, n_full, False)
    run(n_full, n_diag, True)

    lam_p = lam_ref[...]
    lam = (jnp.exp(jnp.sum(lam_p[0:1] * lam_p[1:2], axis=-1, keepdims=True))
           - jnp.exp(jnp.sum(lam_p[2:3] * lam_p[3:4], axis=-1, keepdims=True)) + lambda_init)
    sw = sw_ref[...] * (1.0 - lambda_init)

    def finish(i, _):
        acc0, acc1 = acc_ref[i, :t], acc_ref[i, t:]
        o = acc0[:, :HEAD_W] / acc0[:, HEAD_W:] - lam * (acc1[:, :HEAD_W] / acc1[:, HEAD_W:])
        o = o * _rms_scale(o, SUBLN_EPS) * sw
        o_ref[pl.ds(pl.multiple_of(i * t, t), t), :] = o.astype(o_ref.dtype)
        return 0

    lax.fori_loop(0, n_q, finish, 0, unroll=2)


def _diff_attention(qkv3, lam_params, subln_w, lambda_init):
    b, s, _ = qkv3.shape
    t = ATTN_BLOCK
    n_q = s // t
    pairs = [(i, j) for i in range(n_q) for j in range(i)] + [(i, i) for i in range(n_q)]
    n_diag, n_full = n_q, len(pairs) - n_q
    qi_tab = jnp.asarray([p[0] for p in pairs], jnp.int32)
    kj_tab = jnp.asarray([p[1] for p in pairs], jnp.int32)
    kern = functools.partial(_attn_kernel, lambda_init=lambda_init, n_full=n_full, n_diag=n_diag)
    head = lambda col0: pl.BlockSpec((None, s, HEAD_W), lambda bi, h, *_: (bi, 0, col0 + h))
    pair_buf = lambda n, dt: [pltpu.VMEM((2 * t, n), dt)] * 2
    return pl.pallas_call(
        kern,
        out_shape=jax.ShapeDtypeStruct((b, s, D_MODEL), BF16),
        grid_spec=pltpu.PrefetchScalarGridSpec(
            num_scalar_prefetch=2,
            grid=(b, HEADS),
            in_specs=[head(0), head(HEADS), head(2 * HEADS),
                      _const_spec((4, MAP_DIM)), _const_spec((1, HEAD_W))],
            out_specs=head(0),
            scratch_shapes=[
                pltpu.VMEM((n_q, 2 * t, HEAD_W), BF16),
                pltpu.VMEM((n_q, 2 * t, HEAD_W), F32),
                pltpu.VMEM((n_q, 2 * t, 2 * HEAD_W), F32),
                *pair_buf(t, F32), *pair_buf(t, BF16), *pair_buf(HEAD_W, F32)],
        ),
        compiler_params=pltpu.CompilerParams(
            dimension_semantics=("parallel", "parallel"), vmem_limit_bytes=VMEM_LIMIT),
        name="diff_flash_attention",
    )(qi_tab, kj_tab, qkv3, qkv3, qkv3, lam_params, subln_w.reshape(1, HEAD_W))


def _split2(x):
    hi = x.astype(BF16)
    return hi, (x - hi.astype(F32)).astype(BF16)


def _silu(x):
    h = 0.5 * x
    return h * jnp.tanh(h) + h


def _hgrn_kernel(q_ref, f_ref, i_ref, g_ref, lbp_ref, gw_ref, o_ref, state_ref, lb_ref, *, layer):
    c = HGRN_CHUNK

    @pl.when(pl.program_id(1) == 0)
    def _():
        state_ref[...] = jnp.zeros_like(state_ref)
        p = lbp_ref[...]
        e = jnp.exp(p - jnp.max(p, axis=0, keepdims=True))
        lb_ref[...] = (jnp.sum(e[1:layer + 1], axis=0, keepdims=True)
                       / jnp.sum(e, axis=0, keepdims=True))

    for r0 in range(0, q_ref.shape[0], c):
        rows = slice(r0, r0 + c)
        _hgrn_chunk(q_ref.at[rows], f_ref.at[rows], i_ref.at[rows], g_ref.at[rows],
                    lb_ref[...], gw_ref[...], o_ref.at[rows], state_ref)


def _hgrn_chunk(q_ref, f_ref, i_ref, g_ref, lb, gw, o_ref, state_ref):
    c = HGRN_CHUNK
    sig = 1.0 / (1.0 + jnp.exp(-f_ref[...].astype(F32)))
    f = lb + (1.0 - lb) * sig
    kk = 1.0 - f
    logf = jnp.log2(f)
    r_idx = lax.broadcasted_iota(jnp.int32, (c, c), 0)
    c_idx = lax.broadcasted_iota(jnp.int32, (c, c), 1)
    causal = c_idx <= r_idx
    tri = causal.astype(BF16)
    cum = sum(jnp.dot(tri, part, preferred_element_type=F32) for part in _split2(logf))
    grp = c // 4
    row = lambda i: cum[i:i + 1]
    by_group = lambda rows: jnp.concatenate(
        [jnp.broadcast_to(r, (grp, r.shape[1])) for r in rows], axis=0)
    mids = [row(s * grp + grp // 2 - 1) for s in range(4)]
    b1, b2, b3, cum_last = row(grp - 1), row(2 * grp - 1), row(3 * grp - 1), row(c - 1)
    mid_rows = by_group(mids)
    q_d = _silu(q_ref[...].astype(F32)) * jnp.exp2(cum - mid_rows)
    k_d = kk * jnp.exp2(mid_rows - cum)
    q_state = (q_d * by_group([jnp.exp2(m) for m in mids])).astype(BF16)
    k_end = (k_d * by_group([jnp.exp2(cum_last - m) for m in mids])).astype(BF16)
    chunk_decay = jnp.exp2(cum_last)
    group = lambda x, s, n=1: x[s * grp:(s + n) * grp]
    zeros = lambda n: jnp.zeros((n * grp, cum.shape[1]), BF16)
    cast = lambda x: x.astype(BF16)
    q_x = [jnp.concatenate([zeros(1), cast(group(q_d, 1) * jnp.exp2(mids[1] - b1)), zeros(2)], axis=0),
           jnp.concatenate([zeros(3), cast(group(q_d, 3) * jnp.exp2(mids[3] - b3))], axis=0),
           jnp.concatenate([zeros(2), cast(group(q_d, 2, 2) * by_group(
               [jnp.exp2(mids[2] - b2), jnp.exp2(mids[3] - b2)]))], axis=0)]
    k_x = [jnp.concatenate([cast(group(k_d, 0) * jnp.exp2(b1 - mids[0])), zeros(3)], axis=0),
           jnp.concatenate([zeros(2), cast(group(k_d, 2) * jnp.exp2(b3 - mids[2])), zeros(1)], axis=0),
           jnp.concatenate([cast(group(k_d, 0, 2) * by_group(
               [jnp.exp2(b2 - mids[0]), jnp.exp2(b2 - mids[1])])), zeros(2)], axis=0)]
    q_intra = q_d.astype(BF16)
    k_intra = k_d.astype(BF16)
    assert grp & (grp - 1) == 0
    same_group = causal & ((r_idx ^ c_idx) < grp)
    v = i_ref[...]
    g = _silu(g_ref[...].astype(F32))
    nt = (((1,), (1,)), ((), ()))
    tn = (((0,), (0,)), ((), ()))
    for h in range(HEADS):
        sl = slice(h * HEAD_W, (h + 1) * HEAD_W)
        st = state_ref[h]
        a_in = lax.dot_general(q_intra[:, sl], k_intra[:, sl], nt, preferred_element_type=F32)
        a_x = lax.dot_general(jnp.concatenate([x[:, sl] for x in q_x], axis=1),
                              jnp.concatenate([x[:, sl] for x in k_x], axis=1),
                              nt, preferred_element_type=F32)
        a = jnp.where(same_group, a_in, a_x).astype(BF16)
        o = (jnp.dot(a, v[:, sl], preferred_element_type=F32)
             + lax.dot_general(q_state[:, sl], st.astype(BF16), nt, preferred_element_type=F32))
        state_ref[h] = chunk_decay[:, sl] * st + lax.dot_general(
            v[:, sl], k_end[:, sl], tn, preferred_element_type=F32)
        o = o * _rms_scale(o, GNORM_EPS) * gw * g[:, sl]
        o_ref[:, sl] = o.astype(o_ref.dtype)


def _hgrn_recurrence(u3, lb_param, gnorm_w, layer):
    b, s, _ = u3.shape
    c = HGRN_STEP_ROWS
    sec = lambda k: pl.BlockSpec((None, c, D_MODEL), lambda bi, n: (bi, n, k))
    return pl.pallas_call(
        functools.partial(_hgrn_kernel, layer=layer),
        out_shape=jax.ShapeDtypeStruct((b, s, D_MODEL), BF16),
        grid=(b, s // c),
        in_specs=[sec(0), sec(1), sec(2), sec(3), _const_spec(lb_param.shape),
                  _const_spec((1, HEAD_W))],
        out_specs=pl.BlockSpec((None, c, D_MODEL), lambda bi, n: (bi, n, 0)),
        scratch_shapes=[pltpu.VMEM((HEADS, HEAD_W, HEAD_W), F32), pltpu.VMEM((1, D_MODEL), F32)],
        compiler_params=pltpu.CompilerParams(
            dimension_semantics=("parallel", "arbitrary"), vmem_limit_bytes=VMEM_LIMIT),
        name="hgrn2_recurrence",
    )(u3, u3, u3, u3, lb_param.astype(F32), gnorm_w.reshape(1, HEAD_W))


def _post_kernel(y_ref, x_ref, wo_ref, nw_ref, win_ref, wout_ref, *rest, final_norm):
    if final_norm:
        fw_ref, o_ref, h_ref = rest
    else:
        o_ref, h_ref = rest
    x = x_ref[...] + jnp.dot(y_ref[...], wo_ref[...], preferred_element_type=F32)
    xn = (x * _rms_scale(x, NORM_EPS) * nw_ref[...]).astype(BF16)
    for c0 in range(0, FFN_HIDDEN, FFN_CHUNK):
        gate = jnp.dot(xn, win_ref[:, c0:c0 + FFN_CHUNK], preferred_element_type=F32)
        up = jnp.dot(xn, win_ref[:, FFN_HIDDEN + c0:FFN_HIDDEN + c0 + FFN_CHUNK],
                     preferred_element_type=F32)
        h_ref[:, c0:c0 + FFN_CHUNK] = (_silu(gate) * up).astype(BF16)
    x = x + jnp.dot(h_ref[...], wout_ref[...], preferred_element_type=F32)
    if final_norm:
        x = x * _rms_scale(x, NORM_EPS) * fw_ref[...]
    o_ref[...] = x


def _post_mixer(y2, x2, w_o, norm_w, w_in, w_out, final_w=None):
    t, d = x2.shape
    row = lambda n: pl.BlockSpec((ROW_TILE, n), lambda i: (i, 0))
    in_specs = [row(d), row(d), _const_spec((d, d)), _const_spec((1, d)),
                _const_spec((d, 2 * FFN_HIDDEN)), _const_spec((FFN_HIDDEN, d))]
    args = [y2, x2, w_o, norm_w.reshape(1, d), w_in, w_out]
    if final_w is not None:
        in_specs.append(_const_spec((1, d)))
        args.append(final_w.reshape(1, d))
    return pl.pallas_call(
        functools.partial(_post_kernel, final_norm=final_w is not None),
        out_shape=jax.ShapeDtypeStruct((t, d), F32),
        grid=(t // ROW_TILE,),
        in_specs=in_specs,
        out_specs=row(d),
        scratch_shapes=[pltpu.VMEM((ROW_TILE, FFN_HIDDEN), BF16)],
        compiler_params=pltpu.CompilerParams(
            dimension_semantics=("parallel",), vmem_limit_bytes=VMEM_LIMIT),
        name="outproj_ffn",
    )(*args)


def _lambda_init(layer_idx):
    return 0.8 - 0.6 * math.exp(-0.3 * layer_idx)


def kernel(x, positions, norm_mix_w, norm_ffn_w, final_norm_w, attn_w_in, attn_w_out, attn_lambda_q1, attn_lambda_k1, attn_lambda_q2, attn_lambda_k2, attn_subln_w, hgrn_w_in, hgrn_w_out, hgrn_gnorm_w, hgrn_lb_param, ffn_w_in, ffn_w_out):
    b, s, d = x.shape
    depth = norm_mix_w.shape[0]
    x2 = x.reshape(b * s, d)
    rope = _rope_tables(positions)
    for i in range(depth):
        j = i // 2
        last = final_norm_w if i == depth - 1 else None
        if i % 2 == 0:
            qkv = _project(x2, norm_mix_w[i], attn_w_in[j].astype(BF16), rope=rope)
            lam_params = jnp.stack([attn_lambda_q1[j], attn_lambda_k1[j],
                                    attn_lambda_q2[j], attn_lambda_k2[j]]).astype(F32)
            y = _diff_attention(qkv.reshape(b, s, 3 * d), lam_params, attn_subln_w[j], _lambda_init(i))
            w_o = attn_w_out[j]
        else:
            u = _project(x2, norm_mix_w[i], hgrn_w_in[j].astype(BF16))
            y = _hgrn_recurrence(u.reshape(b, s, 4 * d), hgrn_lb_param, hgrn_gnorm_w[j], i)
            w_o = hgrn_w_out[j]
        x2 = _post_mixer(y.reshape(b * s, d), x2, w_o.astype(BF16), norm_ffn_w[i],
                         ffn_w_in[i].astype(BF16), ffn_w_out[i].astype(BF16), last)
    return x2.reshape(b, s, d)
```

```python
import functools
import math

import jax
import jax.numpy as jnp
from jax import lax
from jax.experimental import pallas as pl
from jax.experimental.pallas import tpu as pltpu

F32 = jnp.float32
BF16 = jnp.bfloat16

D_MODEL = 1024
HEADS = 8
HEAD_W = 128
MAP_DIM = 64
ROPE_HALF = MAP_DIM // 2
ROPE_THETA = 10000.0
FFN_HIDDEN = 2816
NORM_EPS = 1e-6
SUBLN_EPS = 1e-5
GNORM_EPS = 1e-6

ROW_TILE = 512
COL_CHUNK = 512
FFN_CHUNK = 256
ATTN_BLOCK = 256
ATTN_UNROLL = 8
HGRN_CHUNK = 128
HGRN_GROUPS = 4
HGRN_STEP_ROWS = 1024
VMEM_LIMIT = 56 * 1024 * 1024
MASK_VALUE = float(jnp.finfo(jnp.float32).min)


def _rms_scale(x, eps):
    return lax.rsqrt(jnp.mean(x * x, axis=-1, keepdims=True) + eps)


def _const_spec(shape):
    return pl.BlockSpec(shape, lambda *_: (0,) * len(shape), pipeline_mode=pl.Buffered(1))


def _rope_table_kernel(pos_ref, invf_ref, cos_ref, sin_ref):
    ang = pos_ref[...].astype(F32) * invf_ref[...]
    lane = lax.broadcasted_iota(jnp.int32, ang.shape, 1)
    first_half = (lane & ROPE_HALF) == 0
    s = jnp.sin(ang)
    cos_ref[...] = jnp.cos(ang)
    sin_ref[...] = jnp.where(first_half, -s, s)


def _rope_tables(positions):
    t = positions.size
    inv_freq = 1.0 / (ROPE_THETA ** (jnp.arange(0, MAP_DIM, 2, dtype=F32) / MAP_DIM))
    invf = jnp.tile(inv_freq, HEAD_W // ROPE_HALF).reshape(1, HEAD_W)
    tm = ROW_TILE
    return pl.pallas_call(
        _rope_table_kernel,
        out_shape=(jax.ShapeDtypeStruct((t, HEAD_W), F32),) * 2,
        grid=(t // tm,),
        in_specs=[pl.BlockSpec((tm, 1), lambda i: (i, 0)), _const_spec((1, HEAD_W))],
        out_specs=(pl.BlockSpec((tm, HEAD_W), lambda i: (i, 0)),) * 2,
        name="rope_tables",
    )(positions.reshape(t, 1), invf)


def _proj_kernel(x_ref, nw_ref, w_ref, *rest, rope_cols, q_cols, q_scale):
    if rope_cols:
        cos_ref, sin_ref, o_ref = rest
    else:
        (o_ref,) = rest
    x = x_ref[...]
    xn = (x * _rms_scale(x, NORM_EPS) * nw_ref[...]).astype(BF16)
    n_out = o_ref.shape[1]
    if rope_cols:
        cos = cos_ref[...]
        sin = sin_ref[...]
        lane = lax.broadcasted_iota(jnp.int32, cos.shape, 1)
        first_half = (lane & ROPE_HALF) == 0
    for c0 in range(0, n_out, COL_CHUNK):
        y = jnp.dot(xn, w_ref[:, c0:c0 + COL_CHUNK], preferred_element_type=F32)
        if c0 < rope_cols:
            for g0 in range(0, COL_CHUNK, HEAD_W):
                yg = y[:, g0:g0 + HEAD_W]
                rot = jnp.where(first_half, pltpu.roll(yg, HEAD_W - ROPE_HALF, 1),
                                pltpu.roll(yg, ROPE_HALF, 1))
                r = yg * cos + rot * sin
                if c0 < q_cols:
                    r = r * q_scale
                o_ref[:, c0 + g0:c0 + g0 + HEAD_W] = r.astype(o_ref.dtype)
        else:
            o_ref[:, c0:c0 + COL_CHUNK] = y.astype(o_ref.dtype)


def _project(x2, norm_w, w, rope=None):
    t, d = x2.shape
    n_out = w.shape[1]
    in_specs = [pl.BlockSpec((ROW_TILE, d), lambda i: (i, 0)),
                _const_spec((1, d)), _const_spec((d, n_out))]
    args = [x2, norm_w.reshape(1, d), w]
    if rope is not None:
        in_specs += [pl.BlockSpec((ROW_TILE, HEAD_W), lambda i: (i, 0))] * 2
        args += list(rope)
    kern = functools.partial(
        _proj_kernel, rope_cols=2 * d if rope is not None else 0,
        q_cols=d, q_scale=MAP_DIM ** -0.5 * math.log2(math.e))
    return pl.pallas_call(
        kern,
        out_shape=jax.ShapeDtypeStruct((t, n_out), BF16),
        grid=(t // ROW_TILE,),
        in_specs=in_specs,
        out_specs=pl.BlockSpec((ROW_TILE, n_out), lambda i: (i, 0)),
        compiler_params=pltpu.CompilerParams(
            dimension_semantics=("parallel",), vmem_limit_bytes=VMEM_LIMIT),
        name="norm_proj_rope" if rope is not None else "norm_proj",
    )(*args)


def _attn_kernel(qi_tab, kj_tab, q_ref, k_ref, v_ref, lam_ref, sw_ref, o_ref,
                 qm_ref, m_ref, acc_ref, s0_ref, s1_ref, p0_ref, p1_ref, a0_ref, a1_ref,
                 *, lambda_init, n_full, n_diag):
    t = ATTN_BLOCK
    n_q = q_ref.shape[0] // t
    nt = (((1,), (1,)), ((), ()))
    s_refs, p_refs, a_refs = (s0_ref, s1_ref), (p0_ref, p1_ref), (a0_ref, a1_ref)

    def split_maps(i, _):
        q = q_ref[pl.ds(pl.multiple_of(i * t, t), t), :]
        lane = lax.broadcasted_iota(jnp.int32, q.shape, 1)
        zero = jnp.zeros_like(q)
        qm_ref[i, :t, :] = jnp.where(lane < MAP_DIM, q, zero)
        qm_ref[i, t:, :] = jnp.where(lane >= MAP_DIM, q, zero)
        m_ref[i] = jnp.full(m_ref.shape[1:], -1e30, F32)
        acc_ref[i] = jnp.zeros(acc_ref.shape[1:], F32)
        return 0

    lax.fori_loop(0, n_q, split_maps, 0)
    ones = jnp.ones((t, HEAD_W), BF16)
    tril = (lax.broadcasted_iota(jnp.int32, (2 * t, t), 1)
            <= (lax.broadcasted_iota(jnp.int32, (2 * t, t), 0) & (t - 1)))

    def scores(b, slot):
        k = k_ref[pl.ds(pl.multiple_of(kj_tab[b] * t, t), t), :]
        s_refs[slot][...] = lax.dot_general(qm_ref[qi_tab[b]], k, nt, preferred_element_type=F32)

    def softmax(b, slot, masked):
        qi = qi_tab[b]
        sc = s_refs[slot][...]
        if masked:
            sc = jnp.where(tril, sc, MASK_VALUE)
        halves = [sc[:, g:g + HEAD_W] for g in range(0, t, HEAD_W)]
        m = m_ref[qi]
        m_new = jnp.maximum(m, jnp.max(functools.reduce(jnp.maximum, halves),
                                       axis=-1, keepdims=True))
        m_ref[qi] = m_new
        a_refs[slot][...] = jnp.exp2(m - m_new)
        p_refs[slot][...] = jnp.concatenate(
            [jnp.exp2(h - m_new).astype(BF16) for h in halves], axis=1)

    def values(b, slot):
        qi = qi_tab[b]
        v_aug = jnp.concatenate([v_ref[pl.ds(pl.multiple_of(kj_tab[b] * t, t), t), :], ones], axis=1)
        pv = jnp.dot(p_refs[slot][...], v_aug, preferred_element_type=F32)
        alpha = a_refs[slot][...]
        for g in range(0, 2 * HEAD_W, HEAD_W):
            acc_ref[qi, :, g:g + HEAD_W] = alpha * acc_ref[qi, :, g:g + HEAD_W] + pv[:, g:g + HEAD_W]

    def run(first, count, masked):
        if count % 2:
            last = first + count - 1
            scores(last, 0)
            softmax(last, 0, masked)
            values(last, 0)
            count -= 1
        if count == 0:
            return

        def tick(b, parity):
            scores(b, parity)
            softmax(b - 1, 1 - parity, masked)
            values(b - 2, parity)

        scores(first, 0)
        scores(first + 1, 1)
        softmax(first, 0, masked)
        unroll = ATTN_UNROLL
        while unroll > 2 and count - 2 < 2 * unroll:
            unroll //= 2
        n_iter = (count - 2) // unroll

        def body(i, _):
            for u in range(unroll):
                tick(first + 2 + unroll * i + u, u % 2)
            return 0

        lax.fori_loop(0, n_iter, body, 0)
        for b in range(first + 2 + unroll * n_iter, first + count):
            tick(b, (b - first) % 2)
        last = first + count - 1
        softmax(last, 1, masked)
        values(last - 1, 0)
        values(last, 1)

    run(0, n_full, False)
    run(n_full, n_diag, True)

    lam_p = lam_ref[...]
    lam = (jnp.exp(jnp.sum(lam_p[0:1] * lam_p[1:2], axis=-1, keepdims=True))
           - jnp.exp(jnp.sum(lam_p[2:3] * lam_p[3:4], axis=-1, keepdims=True)) + lambda_init)
    sw = sw_ref[...] * (1.0 - lambda_init)

    def finish(i, _):
        acc0, acc1 = acc_ref[i, :t], acc_ref[i, t:]
        o = acc0[:, :HEAD_W] / acc0[:, HEAD_W:] - lam * (acc1[:, :HEAD_W] / acc1[:, HEAD_W:])
        o = o * _rms_scale(o, SUBLN_EPS) * sw
        o_ref[pl.ds(pl.multiple_of(i * t, t), t), :] = o.astype(o_ref.dtype)
        return 0

    lax.fori_loop(0, n_q, finish, 0, unroll=2)


def _diff_attention(qkv3, lam_params, subln_w, lambda_init):
    b, s, _ = qkv3.shape
    t = ATTN_BLOCK
    n_q = s // t
    pairs = [(i, j) for i in range(n_q) for j in range(i)] + [(i, i) for i in range(n_q)]
    n_diag, n_full = n_q, len(pairs) - n_q
    qi_tab = jnp.asarray([p[0] for p in pairs], jnp.int32)
    kj_tab = jnp.asarray([p[1] for p in pairs], jnp.int32)
    kern = functools.partial(_attn_kernel, lambda_init=lambda_init, n_full=n_full, n_diag=n_diag)
    head = lambda col0: pl.BlockSpec((None, s, HEAD_W), lambda bi, h, *_: (bi, 0, col0 + h))
    pair_buf = lambda n, dt: [pltpu.VMEM((2 * t, n), dt)] * 2
    return pl.pallas_call(
        kern,
        out_shape=jax.ShapeDtypeStruct((b, s, D_MODEL), BF16),
        grid_spec=pltpu.PrefetchScalarGridSpec(
            num_scalar_prefetch=2,
            grid=(b, HEADS),
            in_specs=[head(0), head(HEADS), head(2 * HEADS),
                      _const_spec((4, MAP_DIM)), _const_spec((1, HEAD_W))],
            out_specs=head(0),
            scratch_shapes=[
                pltpu.VMEM((n_q, 2 * t, HEAD_W), BF16),
                pltpu.VMEM((n_q, 2 * t, HEAD_W), F32),
                pltpu.VMEM((n_q, 2 * t, 2 * HEAD_W), F32),
                *pair_buf(t, F32), *pair_buf(t, BF16), *pair_buf(HEAD_W, F32)],
        ),
        compiler_params=pltpu.CompilerParams(
            dimension_semantics=("parallel", "parallel"), vmem_limit_bytes=VMEM_LIMIT),
        name="diff_flash_attention",
    )(qi_tab, kj_tab, qkv3, qkv3, qkv3, lam_params, subln_w.reshape(1, HEAD_W))


def _split2(x):
    hi = x.astype(BF16)
    return hi, (x - hi.astype(F32)).astype(BF16)


def _silu(x):
    h = 0.5 * x
    return h * jnp.tanh(h) + h


def _hgrn_kernel(q_ref, f_ref, i_ref, g_ref, lbp_ref, gw_ref, tri_ref, o_ref, state_ref, lb_ref, *, layer):
    c = HGRN_CHUNK
    r_idx = lax.broadcasted_iota(jnp.int32, (c, c), 0)
    c_idx = lax.broadcasted_iota(jnp.int32, (c, c), 1)
    grp = c // HGRN_GROUPS
    assert grp & (grp - 1) == 0
    same_group = (c_idx <= r_idx) & ((r_idx ^ c_idx) < grp)

    @pl.when(pl.program_id(1) == 0)
    def _():
        state_ref[...] = jnp.zeros_like(state_ref)
        p = lbp_ref[...]
        e = jnp.exp(p - jnp.max(p, axis=0, keepdims=True))
        lb_ref[...] = (jnp.sum(e[1:layer + 1], axis=0, keepdims=True)
                       / jnp.sum(e, axis=0, keepdims=True))

    chunks = [slice(r0, r0 + c) for r0 in range(0, q_ref.shape[0], c)]
    gates = [_hgrn_gates(f_ref.at[rows], lb_ref[...], tri_ref) for rows in chunks]
    for rows, (kk, cum) in zip(chunks, gates):
        _hgrn_chunk(q_ref.at[rows], i_ref.at[rows], g_ref.at[rows], kk, cum,
                    gw_ref[...], same_group, o_ref.at[rows], state_ref)


def _hgrn_gates(f_ref, lb, tri_ref):
    sig = 1.0 / (1.0 + jnp.exp(-f_ref[...].astype(F32)))
    f = lb + (1.0 - lb) * sig
    logf = jnp.log2(f)
    cum = jnp.dot(tri_ref[...], jnp.concatenate(_split2(logf), axis=0), preferred_element_type=F32)
    return 1.0 - f, cum


def _hgrn_chunk(q_ref, i_ref, g_ref, kk, cum, gw, same_group, o_ref, state_ref):
    c = HGRN_CHUNK
    assert HGRN_GROUPS == 4
    grp = c // HGRN_GROUPS
    row = lambda i: cum[i:i + 1]
    by_group = lambda rows: jnp.concatenate(
        [jnp.broadcast_to(r, (grp, r.shape[1])) for r in rows], axis=0)
    mids = [row(s * grp + grp // 2 - 1) for s in range(4)]
    b1, b2, b3, cum_last = row(grp - 1), row(2 * grp - 1), row(3 * grp - 1), row(c - 1)
    mid_rows = by_group(mids)
    q_d = _silu(q_ref[...].astype(F32)) * jnp.exp2(cum - mid_rows)
    k_d = kk * jnp.exp2(mid_rows - cum)
    q_state = (q_d * by_group([jnp.exp2(m) for m in mids])).astype(BF16)
    k_end = (k_d * by_group([jnp.exp2(cum_last - m) for m in mids])).astype(BF16)
    chunk_decay = jnp.exp2(cum_last)
    group = lambda x, s, n=1: x[s * grp:(s + n) * grp]
    zeros = lambda n: jnp.zeros((n * grp, cum.shape[1]), BF16)
    cast = lambda x: x.astype(BF16)
    q_x = [jnp.concatenate([zeros(1), cast(group(q_d, 1) * jnp.exp2(mids[1] - b1)), zeros(2)], axis=0),
           jnp.concatenate([zeros(3), cast(group(q_d, 3) * jnp.exp2(mids[3] - b3))], axis=0),
           jnp.concatenate([zeros(2), cast(group(q_d, 2, 2) * by_group(
               [jnp.exp2(mids[2] - b2), jnp.exp2(mids[3] - b2)]))], axis=0)]
    k_x = [jnp.concatenate([cast(group(k_d, 0) * jnp.exp2(b1 - mids[0])), zeros(3)], axis=0),
           jnp.concatenate([zeros(2), cast(group(k_d, 2) * jnp.exp2(b3 - mids[2])), zeros(1)], axis=0),
           jnp.concatenate([cast(group(k_d, 0, 2) * by_group(
               [jnp.exp2(b2 - mids[0]), jnp.exp2(b2 - mids[1])])), zeros(2)], axis=0)]
    q_intra = q_d.astype(BF16)
    k_intra = k_d.astype(BF16)
    v = i_ref[...]
    g = _silu(g_ref[...].astype(F32))
    nt = (((1,), (1,)), ((), ()))
    tn = (((0,), (0,)), ((), ()))
    heads = [slice(h * HEAD_W, (h + 1) * HEAD_W) for h in range(HEADS)]
    a_in = [lax.dot_general(q_intra[:, sl], k_intra[:, sl], nt, preferred_element_type=F32)
            for sl in heads]
    a_x = [lax.dot_general(jnp.concatenate([x[:, sl] for x in q_x], axis=1),
                           jnp.concatenate([x[:, sl] for x in k_x], axis=1),
                           nt, preferred_element_type=F32)
           for sl in heads]
    outs = []
    for h, sl in enumerate(heads):
        st = state_ref[h]
        a = jnp.where(same_group, a_in[h], a_x[h]).astype(BF16)
        outs.append(jnp.dot(a, v[:, sl], preferred_element_type=F32)
                    + lax.dot_general(q_state[:, sl], st.astype(BF16), nt, preferred_element_type=F32))
        state_ref[h] = chunk_decay[:, sl] * st + lax.dot_general(
            v[:, sl], k_end[:, sl], tn, preferred_element_type=F32)
    for o, sl in zip(outs, heads):
        o = o * _rms_scale(o, GNORM_EPS) * gw * g[:, sl]
        o_ref[:, sl] = o.astype(o_ref.dtype)


def _hgrn_recurrence(u3, lb_param, gnorm_w, layer):
    b, s, _ = u3.shape
    c = HGRN_STEP_ROWS
    sec = lambda k: pl.BlockSpec((None, c, D_MODEL), lambda bi, n: (bi, n, k))
    return pl.pallas_call(
        functools.partial(_hgrn_kernel, layer=layer),
        out_shape=jax.ShapeDtypeStruct((b, s, D_MODEL), BF16),
        grid=(b, s // c),
        in_specs=[sec(0), sec(1), sec(2), sec(3), _const_spec(lb_param.shape),
                  _const_spec((1, HEAD_W)), _const_spec((HGRN_CHUNK, 2 * HGRN_CHUNK))],
        out_specs=pl.BlockSpec((None, c, D_MODEL), lambda bi, n: (bi, n, 0)),
        scratch_shapes=[pltpu.VMEM((HEADS, HEAD_W, HEAD_W), F32), pltpu.VMEM((1, D_MODEL), F32)],
        compiler_params=pltpu.CompilerParams(
            dimension_semantics=("parallel", "arbitrary"), vmem_limit_bytes=VMEM_LIMIT),
        name="hgrn2_recurrence",
    )(u3, u3, u3, u3, lb_param.astype(F32), gnorm_w.reshape(1, HEAD_W),
      jnp.tile(jnp.tril(jnp.ones((HGRN_CHUNK, HGRN_CHUNK), BF16)), (1, 2)))


def _post_kernel(y_ref, x_ref, wo_ref, nw_ref, win_ref, wout_ref, *rest, final_norm):
    if final_norm:
        fw_ref, o_ref, h_ref = rest
    else:
        o_ref, h_ref = rest
    x = x_ref[...] + jnp.dot(y_ref[...], wo_ref[...], preferred_element_type=F32)
    xn = (x * _rms_scale(x, NORM_EPS) * nw_ref[...]).astype(BF16)
    for c0 in range(0, FFN_HIDDEN, FFN_CHUNK):
        gate = jnp.dot(xn, win_ref[:, c0:c0 + FFN_CHUNK], preferred_element_type=F32)
        up = jnp.dot(xn, win_ref[:, FFN_HIDDEN + c0:FFN_HIDDEN + c0 + FFN_CHUNK],
                     preferred_element_type=F32)
        h_ref[:, c0:c0 + FFN_CHUNK] = (_silu(gate) * up).astype(BF16)
    x = x + jnp.dot(h_ref[...], wout_ref[...], preferred_element_type=F32)
    if final_norm:
        x = x * _rms_scale(x, NORM_EPS) * fw_ref[...]
    o_ref[...] = x


def _post_mixer(y2, x2, w_o, norm_w, w_in, w_out, final_w=None):
    t, d = x2.shape
    row = lambda n: pl.BlockSpec((ROW_TILE, n), lambda i: (i, 0))
    in_specs = [row(d), row(d), _const_spec((d, d)), _const_spec((1, d)),
                _const_spec((d, 2 * FFN_HIDDEN)), _const_spec((FFN_HIDDEN, d))]
    args = [y2, x2, w_o, norm_w.reshape(1, d), w_in, w_out]
    if final_w is not None:
        in_specs.append(_const_spec((1, d)))
        args.append(final_w.reshape(1, d))
    return pl.pallas_call(
        functools.partial(_post_kernel, final_norm=final_w is not None),
        out_shape=jax.ShapeDtypeStruct((t, d), F32),
        grid=(t // ROW_TILE,),
        in_specs=in_specs,
        out_specs=row(d),
        scratch_shapes=[pltpu.VMEM((ROW_TILE, FFN_HIDDEN), BF16)],
        compiler_params=pltpu.CompilerParams(
            dimension_semantics=("parallel",), vmem_limit_bytes=VMEM_LIMIT),
        name="outproj_ffn",
    )(*args)


def _lambda_init(layer_idx):
    return 0.8 - 0.6 * math.exp(-0.3 * layer_idx)


def kernel(x, positions, norm_mix_w, norm_ffn_w, final_norm_w, attn_w_in, attn_w_out, attn_lambda_q1, attn_lambda_k1, attn_lambda_q2, attn_lambda_k2, attn_subln_w, hgrn_w_in, hgrn_w_out, hgrn_gnorm_w, hgrn_lb_param, ffn_w_in, ffn_w_out):
    b, s, d = x.shape
    depth = norm_mix_w.shape[0]
    x2 = x.reshape(b * s, d)
    rope = _rope_tables(positions)
    for i in range(depth):
        j = i // 2
        last = final_norm_w if i == depth - 1 else None
        if i % 2 == 0:
            qkv = _project(x2, norm_mix_w[i], attn_w_in[j].astype(BF16), rope=rope)
            lam_params = jnp.stack([attn_lambda_q1[j], attn_lambda_k1[j],
                                    attn_lambda_q2[j], attn_lambda_k2[j]]).astype(F32)
            y = _diff_attention(qkv.reshape(b, s, 3 * d), lam_params, attn_subln_w[j], _lambda_init(i))
            w_o = attn_w_out[j]
        else:
            u = _project(x2, norm_mix_w[i], hgrn_w_in[j].astype(BF16))
            y = _hgrn_recurrence(u.reshape(b, s, 4 * d), hgrn_lb_param, hgrn_gnorm_w[j], i)
            w_o = hgrn_w_out[j]
        x2 = _post_mixer(y.reshape(b * s, d), x2, w_o.astype(BF16), norm_ffn_w[i],
                         ffn_w_in[i].astype(BF16), ffn_w_out[i].astype(BF16), last)
    return x2.reshape(b, s, d)
```

```python
import functools
import math

import jax
import jax.numpy as jnp
from jax import lax
from jax.experimental import pallas as pl
from jax.experimental.pallas import tpu as pltpu

F32 = jnp.float32
BF16 = jnp.bfloat16

D_MODEL = 1024
HEADS = 8
HEAD_W = 128
MAP_DIM = 64
ROPE_HALF = MAP_DIM // 2
ROPE_THETA = 10000.0
FFN_HIDDEN = 2816
NORM_EPS = 1e-6
SUBLN_EPS = 1e-5
GNORM_EPS = 1e-6

ROW_TILE = 1024
COL_CHUNK = 512
FFN_CHUNK = 256
ATTN_BLOCK = 256
ATTN_UNROLL = 8
HGRN_CHUNK = 128
HGRN_GROUPS = 4
HGRN_STEP_ROWS = 1024
VMEM_LIMIT = 56 * 1024 * 1024
MASK_VALUE = float(jnp.finfo(jnp.float32).min)


def _rms_scale(x, eps):
    return lax.rsqrt(jnp.mean(x * x, axis=-1, keepdims=True) + eps)


def _row_halves(n):
    return [slice(0, n // 2), slice(n // 2, n)]


def _const_spec(shape):
    return pl.BlockSpec(shape, lambda *_: (0,) * len(shape), pipeline_mode=pl.Buffered(1))


def _rope_table_kernel(pos_ref, invf_ref, cos_ref, sin_ref):
    ang = pos_ref[...].astype(F32) * invf_ref[...]
    lane = lax.broadcasted_iota(jnp.int32, ang.shape, 1)
    first_half = (lane & ROPE_HALF) == 0
    s = jnp.sin(ang)
    cos_ref[...] = jnp.cos(ang)
    sin_ref[...] = jnp.where(first_half, -s, s)


def _rope_tables(positions):
    t = positions.size
    inv_freq = 1.0 / (ROPE_THETA ** (jnp.arange(0, MAP_DIM, 2, dtype=F32) / MAP_DIM))
    invf = jnp.tile(inv_freq, HEAD_W // ROPE_HALF).reshape(1, HEAD_W)
    tm = ROW_TILE
    return pl.pallas_call(
        _rope_table_kernel,
        out_shape=(jax.ShapeDtypeStruct((t, HEAD_W), F32),) * 2,
        grid=(t // tm,),
        in_specs=[pl.BlockSpec((tm, 1), lambda i: (i, 0)), _const_spec((1, HEAD_W))],
        out_specs=(pl.BlockSpec((tm, HEAD_W), lambda i: (i, 0)),) * 2,
        name="rope_tables",
    )(positions.reshape(t, 1), invf)


def _proj_kernel(x_ref, nw_ref, w_ref, *rest, rope_cols, q_cols, q_scale):
    if rope_cols:
        cos_ref, sin_ref, o_ref = rest
    else:
        (o_ref,) = rest
    n_out = o_ref.shape[1]
    parts = _row_halves(x_ref.shape[0])
    xn = []
    for rows in parts:
        x = x_ref[rows, :]
        xn.append((x * _rms_scale(x, NORM_EPS) * nw_ref[...]).astype(BF16))
    for rows, xh in zip(parts, xn):
        if rope_cols:
            cos = cos_ref[rows, :]
            sin = sin_ref[rows, :]
            lane = lax.broadcasted_iota(jnp.int32, cos.shape, 1)
            first_half = (lane & ROPE_HALF) == 0
        for c0 in range(0, n_out, COL_CHUNK):
            y = jnp.dot(xh, w_ref[:, c0:c0 + COL_CHUNK], preferred_element_type=F32)
            if c0 < rope_cols:
                for g0 in range(0, COL_CHUNK, HEAD_W):
                    yg = y[:, g0:g0 + HEAD_W]
                    rot = jnp.where(first_half, pltpu.roll(yg, HEAD_W - ROPE_HALF, 1),
                                    pltpu.roll(yg, ROPE_HALF, 1))
                    r = yg * cos + rot * sin
                    if c0 < q_cols:
                        r = r * q_scale
                    o_ref[rows, c0 + g0:c0 + g0 + HEAD_W] = r.astype(o_ref.dtype)
            else:
                o_ref[rows, c0:c0 + COL_CHUNK] = y.astype(o_ref.dtype)


def _project(x2, norm_w, w, rope=None):
    t, d = x2.shape
    n_out = w.shape[1]
    in_specs = [pl.BlockSpec((ROW_TILE, d), lambda i: (i, 0)),
                _const_spec((1, d)), _const_spec((d, n_out))]
    args = [x2, norm_w.reshape(1, d), w]
    if rope is not None:
        in_specs += [pl.BlockSpec((ROW_TILE, HEAD_W), lambda i: (i, 0))] * 2
        args += list(rope)
    kern = functools.partial(
        _proj_kernel, rope_cols=2 * d if rope is not None else 0,
        q_cols=d, q_scale=MAP_DIM ** -0.5 * math.log2(math.e))
    return pl.pallas_call(
        kern,
        out_shape=jax.ShapeDtypeStruct((t, n_out), BF16),
        grid=(t // ROW_TILE,),
        in_specs=in_specs,
        out_specs=pl.BlockSpec((ROW_TILE, n_out), lambda i: (i, 0)),
        compiler_params=pltpu.CompilerParams(
            dimension_semantics=("parallel",), vmem_limit_bytes=VMEM_LIMIT),
        name="norm_proj_rope" if rope is not None else "norm_proj",
    )(*args)


def _attn_kernel(qi_tab, kj_tab, q_ref, k_ref, v_ref, lam_ref, sw_ref, o_ref,
                 qm_ref, m_ref, acc_ref, s0_ref, s1_ref, p0_ref, p1_ref, a0_ref, a1_ref,
                 *, lambda_init, n_full, n_diag):
    t = ATTN_BLOCK
    n_q = q_ref.shape[0] // t
    nt = (((1,), (1,)), ((), ()))
    s_refs, p_refs, a_refs = (s0_ref, s1_ref), (p0_ref, p1_ref), (a0_ref, a1_ref)

    def split_maps(i, _):
        q = q_ref[pl.ds(pl.multiple_of(i * t, t), t), :]
        lane = lax.broadcasted_iota(jnp.int32, q.shape, 1)
        zero = jnp.zeros_like(q)
        qm_ref[i, :t, :] = jnp.where(lane < MAP_DIM, q, zero)
        qm_ref[i, t:, :] = jnp.where(lane >= MAP_DIM, q, zero)
        m_ref[i] = jnp.full(m_ref.shape[1:], -1e30, F32)
        acc_ref[i] = jnp.zeros(acc_ref.shape[1:], F32)
        return 0

    lax.fori_loop(0, n_q, split_maps, 0)
    ones = jnp.ones((t, HEAD_W), BF16)
    tril = (lax.broadcasted_iota(jnp.int32, (2 * t, t), 1)
            <= (lax.broadcasted_iota(jnp.int32, (2 * t, t), 0) & (t - 1)))

    def scores(b, slot):
        k = k_ref[pl.ds(pl.multiple_of(kj_tab[b] * t, t), t), :]
        s_refs[slot][...] = lax.dot_general(qm_ref[qi_tab[b]], k, nt, preferred_element_type=F32)

    def softmax(b, slot, masked):
        qi = qi_tab[b]
        sc = s_refs[slot][...]
        if masked:
            sc = jnp.where(tril, sc, MASK_VALUE)
        halves = [sc[:, g:g + HEAD_W] for g in range(0, t, HEAD_W)]
        m = m_ref[qi]
        m_new = jnp.maximum(m, jnp.max(functools.reduce(jnp.maximum, halves),
                                       axis=-1, keepdims=True))
        m_ref[qi] = m_new
        a_refs[slot][...] = jnp.exp2(m - m_new)
        p_refs[slot][...] = jnp.concatenate(
            [jnp.exp2(h - m_new).astype(BF16) for h in halves], axis=1)

    def values(b, slot):
        qi = qi_tab[b]
        v_aug = jnp.concatenate([v_ref[pl.ds(pl.multiple_of(kj_tab[b] * t, t), t), :], ones], axis=1)
        pv = jnp.dot(p_refs[slot][...], v_aug, preferred_element_type=F32)
        alpha = a_refs[slot][...]
        for g in range(0, 2 * HEAD_W, HEAD_W):
            acc_ref[qi, :, g:g + HEAD_W] = alpha * acc_ref[qi, :, g:g + HEAD_W] + pv[:, g:g + HEAD_W]

    def run(first, count, masked):
        if count % 2:
            last = first + count - 1
            scores(last, 0)
            softmax(last, 0, masked)
            values(last, 0)
            count -= 1
        if count == 0:
            return

        def tick(b, parity):
            scores(b, parity)
            softmax(b - 1, 1 - parity, masked)
            values(b - 2, parity)

        scores(first, 0)
        scores(first + 1, 1)
        softmax(first, 0, masked)
        unroll = ATTN_UNROLL
        while unroll > 2 and count - 2 < 2 * unroll:
            unroll //= 2
        n_iter = (count - 2) // unroll

        def body(i, _):
            for u in range(unroll):
                tick(first + 2 + unroll * i + u, u % 2)
            return 0

        lax.fori_loop(0, n_iter, body, 0)
        for b in range(first + 2 + unroll * n_iter, first + count):
            tick(b, (b - first) % 2)
        last = first + count - 1
        softmax(last, 1, masked)
        values(last - 1, 0)
        values(last, 1)

    run(0, n_full, False)
    run(n_full, n_diag, True)

    lam_p = lam_ref[...]
    lam = (jnp.exp(jnp.sum(lam_p[0:1] * lam_p[1:2], axis=-1, keepdims=True))
           - jnp.exp(jnp.sum(lam_p[2:3] * lam_p[3:4], axis=-1, keepdims=True)) + lambda_init)
    sw = sw_ref[...] * (1.0 - lambda_init)

    def finish(i, _):
        acc0, acc1 = acc_ref[i, :t], acc_ref[i, t:]
        o = acc0[:, :HEAD_W] / acc0[:, HEAD_W:] - lam * (acc1[:, :HEAD_W] / acc1[:, HEAD_W:])
        o = o * _rms_scale(o, SUBLN_EPS) * sw
        o_ref[pl.ds(pl.multiple_of(i * t, t), t), :] = o.astype(o_ref.dtype)
        return 0

    lax.fori_loop(0, n_q, finish, 0, unroll=4)


def _diff_attention(qkv3, lam_params, subln_w, lambda_init):
    b, s, _ = qkv3.shape
    t = ATTN_BLOCK
    n_q = s // t
    pairs = [(i, j) for i in range(n_q) for j in range(i)] + [(i, i) for i in range(n_q)]
    n_diag, n_full = n_q, len(pairs) - n_q
    qi_tab = jnp.asarray([p[0] for p in pairs], jnp.int32)
    kj_tab = jnp.asarray([p[1] for p in pairs], jnp.int32)
    kern = functools.partial(_attn_kernel, lambda_init=lambda_init, n_full=n_full, n_diag=n_diag)
    head = lambda col0: pl.BlockSpec((None, s, HEAD_W), lambda bi, h, *_: (bi, 0, col0 + h))
    pair_buf = lambda n, dt: [pltpu.VMEM((2 * t, n), dt)] * 2
    return pl.pallas_call(
        kern,
        out_shape=jax.ShapeDtypeStruct((b, s, D_MODEL), BF16),
        grid_spec=pltpu.PrefetchScalarGridSpec(
            num_scalar_prefetch=2,
            grid=(b, HEADS),
            in_specs=[head(0), head(HEADS), head(2 * HEADS),
                      _const_spec((4, MAP_DIM)), _const_spec((1, HEAD_W))],
            out_specs=head(0),
            scratch_shapes=[
                pltpu.VMEM((n_q, 2 * t, HEAD_W), BF16),
                pltpu.VMEM((n_q, 2 * t, HEAD_W), F32),
                pltpu.VMEM((n_q, 2 * t, 2 * HEAD_W), F32),
                *pair_buf(t, F32), *pair_buf(t, BF16), *pair_buf(HEAD_W, F32)],
        ),
        compiler_params=pltpu.CompilerParams(
            dimension_semantics=("parallel", "parallel"), vmem_limit_bytes=VMEM_LIMIT),
        name="diff_flash_attention",
    )(qi_tab, kj_tab, qkv3, qkv3, qkv3, lam_params, subln_w.reshape(1, HEAD_W))


def _split2(x):
    hi = x.astype(BF16)
    return hi, (x - hi.astype(F32)).astype(BF16)


def _silu(x):
    h = 0.5 * x
    return h * jnp.tanh(h) + h


def _hgrn_kernel(q_ref, f_ref, i_ref, g_ref, lbp_ref, gw_ref, tri_ref, o_ref, state_ref, lb_ref, *, layer):
    c = HGRN_CHUNK
    r_idx = lax.broadcasted_iota(jnp.int32, (c, c), 0)
    c_idx = lax.broadcasted_iota(jnp.int32, (c, c), 1)
    grp = c // HGRN_GROUPS
    assert grp & (grp - 1) == 0
    same_group = (c_idx <= r_idx) & ((r_idx ^ c_idx) < grp)

    @pl.when(pl.program_id(1) == 0)
    def _():
        state_ref[...] = jnp.zeros_like(state_ref)
        p = lbp_ref[...]
        e = jnp.exp(p - jnp.max(p, axis=0, keepdims=True))
        lb_ref[...] = (jnp.sum(e[1:layer + 1], axis=0, keepdims=True)
                       / jnp.sum(e, axis=0, keepdims=True))

    chunks = [slice(r0, r0 + c) for r0 in range(0, q_ref.shape[0], c)]
    gates = [_hgrn_gates(f_ref.at[rows], lb_ref[...], tri_ref) for rows in chunks]
    for rows, (kk, cum) in zip(chunks, gates):
        _hgrn_chunk(q_ref.at[rows], i_ref.at[rows], g_ref.at[rows], kk, cum,
                    gw_ref[...], same_group, o_ref.at[rows], state_ref)


def _hgrn_gates(f_ref, lb, tri_ref):
    sig = 1.0 / (1.0 + jnp.exp(-f_ref[...].astype(F32)))
    f = lb + (1.0 - lb) * sig
    logf = jnp.log2(f)
    cum = jnp.dot(tri_ref[...], jnp.concatenate(_split2(logf), axis=0), preferred_element_type=F32)
    return 1.0 - f, cum


def _hgrn_chunk(q_ref, i_ref, g_ref, kk, cum, gw, same_group, o_ref, state_ref):
    c = HGRN_CHUNK
    assert HGRN_GROUPS == 4
    grp = c // HGRN_GROUPS
    row = lambda i: cum[i:i + 1]
    by_group = lambda rows: jnp.concatenate(
        [jnp.broadcast_to(r, (grp, r.shape[1])) for r in rows], axis=0)
    mids = [row(s * grp + grp // 2 - 1) for s in range(4)]
    b1, b2, b3, cum_last = row(grp - 1), row(2 * grp - 1), row(3 * grp - 1), row(c - 1)
    mid_rows = by_group(mids)
    q_d = _silu(q_ref[...].astype(F32)) * jnp.exp2(cum - mid_rows)
    k_d = kk * jnp.exp2(mid_rows - cum)
    q_state = (q_d * by_group([jnp.exp2(m) for m in mids])).astype(BF16)
    k_end = (k_d * by_group([jnp.exp2(cum_last - m) for m in mids])).astype(BF16)
    chunk_decay = jnp.exp2(cum_last)
    group = lambda x, s, n=1: x[s * grp:(s + n) * grp]
    zeros = lambda n: jnp.zeros((n * grp, cum.shape[1]), BF16)
    cast = lambda x: x.astype(BF16)
    q_x = [jnp.concatenate([zeros(1), cast(group(q_d, 1) * jnp.exp2(mids[1] - b1)), zeros(2)], axis=0),
           jnp.concatenate([zeros(3), cast(group(q_d, 3) * jnp.exp2(mids[3] - b3))], axis=0),
           jnp.concatenate([zeros(2), cast(group(q_d, 2, 2) * by_group(
               [jnp.exp2(mids[2] - b2), jnp.exp2(mids[3] - b2)]))], axis=0)]
    k_x = [jnp.concatenate([cast(group(k_d, 0) * jnp.exp2(b1 - mids[0])), zeros(3)], axis=0),
           jnp.concatenate([zeros(2), cast(group(k_d, 2) * jnp.exp2(b3 - mids[2])), zeros(1)], axis=0),
           jnp.concatenate([cast(group(k_d, 0, 2) * by_group(
               [jnp.exp2(b2 - mids[0]), jnp.exp2(b2 - mids[1])])), zeros(2)], axis=0)]
    q_intra = q_d.astype(BF16)
    k_intra = k_d.astype(BF16)
    v = i_ref[...]
    g = _silu(g_ref[...].astype(F32))
    nt = (((1,), (1,)), ((), ()))
    tn = (((0,), (0,)), ((), ()))
    heads = [slice(h * HEAD_W, (h + 1) * HEAD_W) for h in range(HEADS)]
    a_in = [lax.dot_general(q_intra[:, sl], k_intra[:, sl], nt, preferred_element_type=F32)
            for sl in heads]
    a_x = [lax.dot_general(jnp.concatenate([x[:, sl] for x in q_x], axis=1),
                           jnp.concatenate([x[:, sl] for x in k_x], axis=1),
                           nt, preferred_element_type=F32)
           for sl in heads]
    outs = []
    for h, sl in enumerate(heads):
        st = state_ref[h]
        a = jnp.where(same_group, a_in[h], a_x[h]).astype(BF16)
        outs.append(jnp.dot(a, v[:, sl], preferred_element_type=F32)
                    + lax.dot_general(q_state[:, sl], st.astype(BF16), nt, preferred_element_type=F32))
        state_ref[h] = chunk_decay[:, sl] * st + lax.dot_general(
            v[:, sl], k_end[:, sl], tn, preferred_element_type=F32)
    for o, sl in zip(outs, heads):
        o = o * _rms_scale(o, GNORM_EPS) * gw * g[:, sl]
        o_ref[:, sl] = o.astype(o_ref.dtype)


def _hgrn_recurrence(u3, lb_param, gnorm_w, layer):
    b, s, _ = u3.shape
    c = HGRN_STEP_ROWS
    sec = lambda k: pl.BlockSpec((None, c, D_MODEL), lambda bi, n: (bi, n, k))
    return pl.pallas_call(
        functools.partial(_hgrn_kernel, layer=layer),
        out_shape=jax.ShapeDtypeStruct((b, s, D_MODEL), BF16),
        grid=(b, s // c),
        in_specs=[sec(0), sec(1), sec(2), sec(3), _const_spec(lb_param.shape),
                  _const_spec((1, HEAD_W)), _const_spec((HGRN_CHUNK, 2 * HGRN_CHUNK))],
        out_specs=pl.BlockSpec((None, c, D_MODEL), lambda bi, n: (bi, n, 0)),
        scratch_shapes=[pltpu.VMEM((HEADS, HEAD_W, HEAD_W), F32), pltpu.VMEM((1, D_MODEL), F32)],
        compiler_params=pltpu.CompilerParams(
            dimension_semantics=("parallel", "arbitrary"), vmem_limit_bytes=VMEM_LIMIT),
        name="hgrn2_recurrence",
    )(u3, u3, u3, u3, lb_param.astype(F32), gnorm_w.reshape(1, HEAD_W),
      jnp.tile(jnp.tril(jnp.ones((HGRN_CHUNK, HGRN_CHUNK), BF16)), (1, 2)))


def _post_kernel(y_ref, x_ref, wo_ref, nw_ref, win_ref, wout_ref, *rest, final_norm):
    if final_norm:
        fw_ref, o_ref, h_ref = rest
    else:
        o_ref, h_ref = rest
    parts = _row_halves(x_ref.shape[0])
    xs = [x_ref[rows, :] + jnp.dot(y_ref[rows, :], wo_ref[...], preferred_element_type=F32)
          for rows in parts]
    xns = [(x * _rms_scale(x, NORM_EPS) * nw_ref[...]).astype(BF16) for x in xs]
    for rows, x, xn in zip(parts, xs, xns):
        for c0 in range(0, FFN_HIDDEN, FFN_CHUNK):
            gate = jnp.dot(xn, win_ref[:, c0:c0 + FFN_CHUNK], preferred_element_type=F32)
            up = jnp.dot(xn, win_ref[:, FFN_HIDDEN + c0:FFN_HIDDEN + c0 + FFN_CHUNK],
                         preferred_element_type=F32)
            h_ref[rows, c0:c0 + FFN_CHUNK] = (_silu(gate) * up).astype(BF16)
        x = x + jnp.dot(h_ref[rows, :], wout_ref[...], preferred_element_type=F32)
        if final_norm:
            x = x * _rms_scale(x, NORM_EPS) * fw_ref[...]
        o_ref[rows, :] = x


def _post_mixer(y2, x2, w_o, norm_w, w_in, w_out, final_w=None):
    t, d = x2.shape
    row = lambda n: pl.BlockSpec((ROW_TILE, n), lambda i: (i, 0))
    in_specs = [row(d), row(d), _const_spec((d, d)), _const_spec((1, d)),
                _const_spec((d, 2 * FFN_HIDDEN)), _const_spec((FFN_HIDDEN, d))]
    args = [y2, x2, w_o, norm_w.reshape(1, d), w_in, w_out]
    if final_w is not None:
        in_specs.append(_const_spec((1, d)))
        args.append(final_w.reshape(1, d))
    return pl.pallas_call(
        functools.partial(_post_kernel, final_norm=final_w is not None),
        out_shape=jax.ShapeDtypeStruct((t, d), F32),
        grid=(t // ROW_TILE,),
        in_specs=in_specs,
        out_specs=row(d),
        scratch_shapes=[pltpu.VMEM((ROW_TILE, FFN_HIDDEN), BF16)],
        compiler_params=pltpu.CompilerParams(
            dimension_semantics=("parallel",), vmem_limit_bytes=VMEM_LIMIT),
        name="outproj_ffn",
    )(*args)


def _lambda_init(layer_idx):
    return 0.8 - 0.6 * math.exp(-0.3 * layer_idx)


def kernel(x, positions, norm_mix_w, norm_ffn_w, final_norm_w, attn_w_in, attn_w_out, attn_lambda_q1, attn_lambda_k1, attn_lambda_q2, attn_lambda_k2, attn_subln_w, hgrn_w_in, hgrn_w_out, hgrn_gnorm_w, hgrn_lb_param, ffn_w_in, ffn_w_out):
    b, s, d = x.shape
    depth = norm_mix_w.shape[0]
    x2 = x.reshape(b * s, d)
    rope = _rope_tables(positions)
    for i in range(depth):
        j = i // 2
        last = final_norm_w if i == depth - 1 else None
        if i % 2 == 0:
            qkv = _project(x2, norm_mix_w[i], attn_w_in[j].astype(BF16), rope=rope)
            lam_params = jnp.stack([attn_lambda_q1[j], attn_lambda_k1[j],
                                    attn_lambda_q2[j], attn_lambda_k2[j]]).astype(F32)
            y = _diff_attention(qkv.reshape(b, s, 3 * d), lam_params, attn_subln_w[j], _lambda_init(i))
            w_o = attn_w_out[j]
        else:
            u = _project(x2, norm_mix_w[i], hgrn_w_in[j].astype(BF16))
            y = _hgrn_recurrence(u.reshape(b, s, 4 * d), hgrn_lb_param, hgrn_gnorm_w[j], i)
            w_o = hgrn_w_out[j]
        x2 = _post_mixer(y.reshape(b * s, d), x2, w_o.astype(BF16), norm_ffn_w[i],
                         ffn_w_in[i].astype(BF16), ffn_w_out[i].astype(BF16), last)
    return x2.reshape(b, s, d)
```
